```python
import jax, jax.numpy as jnp
from jax import lax
import numpy as np

D_MODEL = 2048
BATCH = 4
SEQ = 2048
DEPTH = 4
DEC_BATCH = 128
DEC_SEQ = 1
PAST_LEN = 16384
PAGE_SIZE = 128

POOL_WIDTH = D_MODEL // 4
POOL_WINDOWS = (2, 4, 8, 16)
POOL_GROUPS = len(POOL_WINDOWS)
POOL_GROUP_DIM = POOL_WIDTH // POOL_GROUPS
POOL_BUF = max(POOL_WINDOWS) - 1
LRU_WIDTH = D_MODEL // 2
LRU_BLOCKS = 8
LRU_BLOCK_DIM = LRU_WIDTH // LRU_BLOCKS
LRU_CONV = 4
LRU_C = 8.0
CONF_WIDTH = D_MODEL // 4
CONF_CONV = 31
D_FF = 4 * D_MODEL
N_BRANCH = 3
IN_COLS = POOL_WIDTH + 2 * LRU_WIDTH + 2 * CONF_WIDTH
EPS = 1e-6

kernel_name = "hybrid_pool_rglru_conformer_step"


def rms_norm(x, g):
    xf = x.astype(jnp.float32)
    y = xf * lax.rsqrt(jnp.mean(xf * xf, axis=-1, keepdims=True) + EPS)
    return (y * g.astype(jnp.float32)).astype(x.dtype)


def layer_norm(x, g, b):
    xf = x.astype(jnp.float32)
    mu = jnp.mean(xf, axis=-1, keepdims=True)
    var = jnp.mean(jnp.square(xf - mu), axis=-1, keepdims=True)
    y = (xf - mu) * lax.rsqrt(var + EPS)
    return (y * g.astype(jnp.float32) + b.astype(jnp.float32)).astype(x.dtype)


def causal_dwconv(buf, x, w, b):
    ext = jnp.concatenate([buf.astype(x.dtype), x], axis=1)
    out = lax.conv_general_dilated(
        ext, w[:, None, :].astype(x.dtype), window_strides=(1,), padding='VALID',
        dimension_numbers=('NWC', 'WIO', 'NWC'), feature_group_count=x.shape[-1])
    k1 = w.shape[0] - 1
    return out + b.astype(x.dtype), ext[:, ext.shape[1] - k1:]


def pool_mixer(u, buf, start_pos, w_grp, scale):
    B, T, P = u.shape
    ext = jnp.concatenate([buf.astype(u.dtype), u], axis=1)
    cs = jnp.cumsum(ext.astype(jnp.float32), axis=1)
    cs = jnp.concatenate([jnp.zeros((B, 1, P), jnp.float32), cs], axis=1)
    end = cs[:, POOL_BUF + 1:POOL_BUF + 1 + T]
    pos = (start_pos + jnp.arange(T)).astype(jnp.float32)
    pooled = []
    for gi, w in enumerate(POOL_WINDOWS):
        sl = slice(gi * POOL_GROUP_DIM, (gi + 1) * POOL_GROUP_DIM)
        s = end[:, :, sl] - cs[:, POOL_BUF + 1 - w:POOL_BUF + 1 - w + T, sl]
        cnt = jnp.minimum(jnp.float32(w), pos + 1.0)
        pooled.append(s / cnt[None, :, None])
    d = jnp.concatenate(pooled, axis=-1) - u.astype(jnp.float32)
    d = d.astype(u.dtype).reshape(B, T, POOL_GROUPS, POOL_GROUP_DIM)
    y = jnp.einsum('btgc,gcd->btgd', d, w_grp).reshape(B, T, P) * scale
    return y, ext[:, ext.shape[1] - POOL_BUF:]


def rg_lru(x, h0, start_pos, w_a, b_a, w_x, b_x, lam):
    B, T, R = x.shape
    xb = x.reshape(B, T, LRU_BLOCKS, LRU_BLOCK_DIM)
    gate_r = jax.nn.sigmoid((jnp.einsum('btnc,ncd->btnd', xb, w_a).reshape(B, T, R) + b_a).astype(jnp.float32))
    gate_i = jax.nn.sigmoid((jnp.einsum('btnc,ncd->btnd', xb, w_x).reshape(B, T, R) + b_x).astype(jnp.float32))
    log_a = -LRU_C * gate_r * jax.nn.softplus(-lam.astype(jnp.float32))
    reset = ((start_pos + jnp.arange(T)) == 0)[None, :, None]
    a = jnp.where(reset, 0.0, jnp.exp(log_a))
    mult = jnp.where(reset, 1.0, jnp.sqrt(-jnp.expm1(2.0 * log_a)))
    bterm = x.astype(jnp.float32) * gate_i * mult
    bterm = bterm.at[:, 0].add(a[:, 0] * h0.astype(jnp.float32))

    def combine(l, r):
        return (l[0] * r[0], r[0] * l[1] + r[1])

    _, h = lax.associative_scan(combine, (a, bterm), axis=1)
    return h, h[:, -1]


def hybrid_layer(x, pool_buf, lru_buf, lru_h, conf_buf, start_pos, W):
    (g_mix, w_in, w_pool_grp, pool_scale, w_pool_br, w_lru_conv, b_lru_conv,
     w_lru_a, b_lru_a, w_lru_x, b_lru_x, lru_lambda, w_lru_br, w_conf_conv,
     b_conf_conv, g_conf, b_conf, w_conf_br, w_gate, b_gate, w_out, g_mlp,
     w_up, w_down) = W
    xn = rms_norm(x, g_mix)
    z = xn @ w_in
    c0 = POOL_WIDTH
    c1 = c0 + LRU_WIDTH
    c2 = c1 + LRU_WIDTH
    c3 = c2 + CONF_WIDTH
    u_pool, u_lru, u_gel, c_a, c_b = z[..., :c0], z[..., c0:c1], z[..., c1:c2], z[..., c2:c3], z[..., c3:]
    y_pool, new_pool = pool_mixer(u_pool, pool_buf, start_pos, w_pool_grp, pool_scale)
    y_pool = y_pool @ w_pool_br
    xc, new_lru_buf = causal_dwconv(lru_buf, u_lru, w_lru_conv, b_lru_conv)
    h, h_last = rg_lru(xc, lru_h, start_pos, w_lru_a, b_lru_a, w_lru_x, b_lru_x, lru_lambda)
    y_lru = (h.astype(x.dtype) * jax.nn.gelu(u_gel, approximate=True)) @ w_lru_br
    v = c_a * jax.nn.sigmoid(c_b)
    vc, new_conf = causal_dwconv(conf_buf, v, w_conf_conv, b_conf_conv)
    y_conf = jax.nn.silu(layer_norm(vc, g_conf, b_conf)) @ w_conf_br
    gates = jax.nn.sigmoid(xn @ w_gate + b_gate)
    g_a, g_b, g_c = jnp.split(gates, N_BRANCH, axis=-1)
    x = x + (g_a * y_pool + g_b * y_lru + g_c * y_conf) @ w_out
    hid = jnp.square(jax.nn.relu(rms_norm(x, g_mlp) @ w_up))
    x = x + hid @ w_down
    return x, new_pool, new_lru_buf, h_last.astype(x.dtype), new_conf


def setup_inputs(seed: int = 0) -> dict:
    key = jax.random.key(seed)
    ks = jax.random.split(key, 40)
    f32 = jnp.float32

    def nrm(k, shape, scale):
        return jax.random.normal(k, shape, f32) * scale

    D, L = D_MODEL, DEPTH
    p = 1.0 / (1.0 + 0.0)
    a0 = jax.random.uniform(ks[0], (L, LRU_WIDTH), f32, 0.9, 0.999)
    s = a0 ** (1.0 / LRU_C)
    lru_lambda = jnp.log(s) - jnp.log1p(-s)
    return {
        "x_prompt": nrm(ks[1], (BATCH, SEQ, D), 1.0),
        "x_sample": nrm(ks[2], (DEC_BATCH, DEC_SEQ, D), 1.0),
        "state_pool": nrm(ks[3], (L, DEC_BATCH, POOL_BUF, POOL_WIDTH), 1.0),
        "state_lru_conv": nrm(ks[4], (L, DEC_BATCH, LRU_CONV - 1, LRU_WIDTH), 1.0),
        "state_lru_h": nrm(ks[5], (L, DEC_BATCH, LRU_WIDTH), 0.5),
        "state_conf_conv": nrm(ks[6], (L, DEC_BATCH, CONF_CONV - 1, CONF_WIDTH), 1.0),
        "g_mix": 1.0 + nrm(ks[7], (L, D), 0.05),
        "w_in": nrm(ks[8], (L, D, IN_COLS), D ** -0.5),
        "w_pool_grp": nrm(ks[9], (L, POOL_GROUPS, POOL_GROUP_DIM, POOL_GROUP_DIM), POOL_GROUP_DIM ** -0.5),
        "pool_scale": 1.0 + nrm(ks[10], (L, POOL_WIDTH), 0.1),
        "w_pool_br": nrm(ks[11], (L, POOL_WIDTH, D), POOL_WIDTH ** -0.5),
        "w_lru_conv": nrm(ks[12], (L, LRU_CONV, LRU_WIDTH), LRU_CONV ** -0.5),
        "b_lru_conv": nrm(ks[13], (L, LRU_WIDTH), 0.02),
        "w_lru_a": nrm(ks[14], (L, LRU_BLOCKS, LRU_BLOCK_DIM, LRU_BLOCK_DIM), LRU_BLOCK_DIM ** -0.5),
        "b_lru_a": nrm(ks[15], (L, LRU_WIDTH), 0.02),
        "w_lru_x": nrm(ks[16], (L, LRU_BLOCKS, LRU_BLOCK_DIM, LRU_BLOCK_DIM), LRU_BLOCK_DIM ** -0.5),
        "b_lru_x": nrm(ks[17], (L, LRU_WIDTH), 0.02),
        "lru_lambda": lru_lambda,
        "w_lru_br": nrm(ks[18], (L, LRU_WIDTH, D), LRU_WIDTH ** -0.5),
        "w_conf_conv": nrm(ks[19], (L, CONF_CONV, CONF_WIDTH), CONF_CONV ** -0.5),
        "b_conf_conv": nrm(ks[20], (L, CONF_WIDTH), 0.02),
        "g_conf": 1.0 + nrm(ks[21], (L, CONF_WIDTH), 0.05),
        "b_conf": nrm(ks[22], (L, CONF_WIDTH), 0.02),
        "w_conf_br": nrm(ks[23], (L, CONF_WIDTH, D), CONF_WIDTH ** -0.5),
        "w_gate": nrm(ks[24], (L, D, N_BRANCH * D), D ** -0.5),
        "b_gate": nrm(ks[25], (L, N_BRANCH * D), 0.02),
        "w_out": nrm(ks[26], (L, D, D), D ** -0.5),
        "g_mlp": 1.0 + nrm(ks[27], (L, D), 0.05),
        "w_up": nrm(ks[28], (L, D, D_FF), D ** -0.5),
        "w_down": nrm(ks[29], (L, D_FF, D), 0.5 * D_FF ** -0.5),
        "g_final": 1.0 + nrm(ks[30], (D,), 0.05),
    }


def reference(x_prompt, x_sample, state_pool, state_lru_conv, state_lru_h, state_conf_conv,
              g_mix, w_in, w_pool_grp, pool_scale, w_pool_br, w_lru_conv, b_lru_conv,
              w_lru_a, b_lru_a, w_lru_x, b_lru_x, lru_lambda, w_lru_br, w_conf_conv,
              b_conf_conv, g_conf, b_conf, w_conf_br, w_gate, b_gate, w_out, g_mlp,
              w_up, w_down, g_final):
    Bp = x_prompt.shape[0]
    dt = x_prompt.dtype
    xp, xs = x_prompt, x_sample
    pool_p, lconv_p, lh_p, cconv_p = [], [], [], []
    pool_s, lconv_s, lh_s, cconv_s = [], [], [], []
    for l in range(DEPTH):
        W = (g_mix[l], w_in[l], w_pool_grp[l], pool_scale[l], w_pool_br[l], w_lru_conv[l],
             b_lru_conv[l], w_lru_a[l], b_lru_a[l], w_lru_x[l], b_lru_x[l], lru_lambda[l],
             w_lru_br[l], w_conf_conv[l], b_conf_conv[l], g_conf[l], b_conf[l], w_conf_br[l],
             w_gate[l], b_gate[l], w_out[l], g_mlp[l], w_up[l], w_down[l])
        xp, a1, a2, a3, a4 = hybrid_layer(
            xp,
            jnp.zeros((Bp, POOL_BUF, POOL_WIDTH), dt),
            jnp.zeros((Bp, LRU_CONV - 1, LRU_WIDTH), dt),
            jnp.zeros((Bp, LRU_WIDTH), dt),
            jnp.zeros((Bp, CONF_CONV - 1, CONF_WIDTH), dt),
            0, W)
        pool_p.append(a1); lconv_p.append(a2); lh_p.append(a3); cconv_p.append(a4)
        xs, b1, b2, b3, b4 = hybrid_layer(
            xs, state_pool[l], state_lru_conv[l], state_lru_h[l], state_conf_conv[l],
            PAST_LEN, W)
        pool_s.append(b1); lconv_s.append(b2); lh_s.append(b3); cconv_s.append(b4)
    y_prompt = rms_norm(xp, g_final)
    y_sample = rms_norm(xs, g_final)
    return (y_prompt, y_sample,
            jnp.stack(pool_p), jnp.stack(lconv_p), jnp.stack(lh_p), jnp.stack(cconv_p),
            jnp.stack(pool_s), jnp.stack(lconv_s), jnp.stack(lh_s), jnp.stack(cconv_s))
```

```python
import functools
import math

import jax
import jax.numpy as jnp
from jax import lax
from jax.experimental import pallas as pl
from jax.experimental.pallas import tpu as pltpu

F32 = jnp.float32
BF16 = jnp.bfloat16

POOL_WINDOWS = (2, 4, 8, 16)
LRU_C = 8.0
EPS = 1e-6
PAST_LEN = 16384
LANE = 128
SUBLANE = 8
VMEM_SLACK = 6 << 20


def _sigmoid(x):
    return 1.0 / (1.0 + jnp.exp(-x))


def _gelu_tanh(x):
    c = math.sqrt(2.0 / math.pi)
    return x * (0.5 * (1.0 + jnp.tanh(c * (x + 0.044715 * (x * x * x)))))


def _softplus(x):
    return jnp.maximum(x, 0.0) + jnp.log1p(jnp.exp(-jnp.abs(x)))


def _params(vmem_bytes, n_axes):
    return pltpu.CompilerParams(
        dimension_semantics=("arbitrary",) * n_axes,
        vmem_limit_bytes=int(vmem_bytes + VMEM_SLACK))


def _inproj_kernel(x_ref, g_ref, w_ref, b_ref, o_ref, xn_ref, *, n_plain):
    n = pl.program_id(1)

    @pl.when(n == 0)
    def _():
        x = x_ref[...]
        ms = jnp.mean(x * x, axis=-1, keepdims=True)
        xn_ref[...] = ((x * lax.rsqrt(ms + EPS)) * g_ref[...]).astype(BF16)

    acc = jnp.dot(xn_ref[...], w_ref[...], preferred_element_type=F32)

    @pl.when(n < n_plain)
    def _():
        o_ref[...] = acc

    @pl.when(n >= n_plain)
    def _():
        o_ref[...] = _sigmoid(acc + b_ref[...])


def _inproj(x, g, w, b, *, n_plain_cols, tm, tn):
    m, d = x.shape
    ncols = w.shape[1]
    vmem = 2 * tm * d * 4 + tm * d * 2 + 2 * d * tn * 2 + 2 * tm * tn * 4
    return pl.pallas_call(
        functools.partial(_inproj_kernel, n_plain=n_plain_cols // tn),
        grid=(m // tm, ncols // tn),
        in_specs=[
            pl.BlockSpec((tm, d), lambda i, j: (i, 0)),
            pl.BlockSpec((1, d), lambda i, j: (0, 0)),
            pl.BlockSpec((d, tn), lambda i, j: (0, j)),
            pl.BlockSpec((1, tn), lambda i, j: (0, j)),
        ],
        out_specs=pl.BlockSpec((tm, tn), lambda i, j: (i, j)),
        out_shape=jax.ShapeDtypeStruct((m, ncols), F32),
        scratch_shapes=[pltpu.VMEM((tm, d), BF16)],
        compiler_params=_params(vmem, 2),
        name="inproj",
    )(x, g, w, b)


def _lru_coeffs(xc, ri, ba, bx, sp):
    gate_r = _sigmoid(ri[:, :LANE] + ba)
    gate_i = _sigmoid(ri[:, LANE:] + bx)
    log_a = (-LRU_C) * gate_r * sp
    a = jnp.exp(log_a)
    th = jnp.tanh(log_a)
    mult = jnp.sqrt((-2.0 * th) / (1.0 - th))
    return a, gate_i, mult


def _layer_norm_silu(vc, g, b):
    mu = jnp.mean(vc, axis=-1, keepdims=True)
    cen = vc - mu
    var = jnp.mean(cen * cen, axis=-1, keepdims=True)
    y = (cen * lax.rsqrt(var + EPS)) * g + b
    return y * _sigmoid(y)


POOL_HALO = 16
LRU_HALO = 8
CONF_HALO = 32
LN_ROWS = 64


def _mixer_prompt_kernel(z_ref, wgrp_ref, pscale_ref, wlc_ref, blc_ref, wax_ref, ba_ref,
                         bx_ref, lam_ref, wcc_ref, bcc_ref, gcf_ref, bcf_ref,
                         f_ref, hlast_ref, vlast_ref,
                         pool_ext, lru_ext, conf_ext, vc_buf, h_carry,
                         *, tt, pool_w, lru_w, conf_w, lru_taps, conf_taps):
    t = pl.program_id(1)
    c_lru = pool_w
    c_gel = c_lru + lru_w
    c_ca = c_gel + lru_w
    c_cb = c_ca + conf_w

    @pl.when(t == 0)
    def _():
        pool_ext[0:POOL_HALO, :] = jnp.zeros((POOL_HALO, pool_w), F32)
        lru_ext[0:LRU_HALO, :] = jnp.zeros((LRU_HALO, lru_w), F32)
        conf_ext[0:CONF_HALO, :] = jnp.zeros((CONF_HALO, conf_w), F32)
        h_carry[...] = jnp.zeros_like(h_carry)

    row = lax.broadcasted_iota(jnp.int32, (tt, LANE), 0)
    pos = row + t * tt
    posf = pos.astype(F32)
    is_first = pos == 0

    pool_ext[POOL_HALO:POOL_HALO + tt, :] = z_ref[:, 0:pool_w]
    for g, w in enumerate(POOL_WINDOWS):
        lanes = slice(g * LANE, (g + 1) * LANE)
        u = pool_ext[POOL_HALO:POOL_HALO + tt, lanes]
        s = u
        for j in range(1, w):
            s = s + pool_ext[POOL_HALO - j:POOL_HALO - j + tt, lanes]
        cnt = jnp.minimum(jnp.float32(w), posf + 1.0)
        d = s / cnt - u
        y = jnp.dot(d.astype(BF16), wgrp_ref[g], preferred_element_type=F32)
        f_ref[:, lanes] = (y * pscale_ref[:, lanes]).astype(BF16)
    pool_ext[0:POOL_HALO, :] = pool_ext[tt:tt + POOL_HALO, :]

    lru_ext[LRU_HALO:LRU_HALO + tt, :] = z_ref[:, c_lru:c_lru + lru_w]
    sp_all = _softplus(-lam_ref[...])
    for n in range(lru_w // LANE):
        lanes = slice(n * LANE, (n + 1) * LANE)
        xc = blc_ref[:, lanes] + wlc_ref[lru_taps - 1:lru_taps, lanes] * lru_ext[LRU_HALO:LRU_HALO + tt, lanes]
        for k in range(lru_taps - 1):
            off = LRU_HALO - (lru_taps - 1) + k
            xc = xc + wlc_ref[k:k + 1, lanes] * lru_ext[off:off + tt, lanes]
        ri = jnp.dot(xc.astype(BF16), wax_ref[n], preferred_element_type=F32)
        a, gate_i, mult = _lru_coeffs(xc, ri, ba_ref[:, lanes], bx_ref[:, lanes], sp_all[:, lanes])
        a = jnp.where(is_first, 0.0, a)
        mult = jnp.where(is_first, 1.0, mult)
        b = xc * gate_i * mult
        k = 1
        while k < tt:
            keep = row >= k
            b_sh = jnp.where(keep, pltpu.roll(b, k, 0), 0.0)
            a_sh = jnp.where(keep, pltpu.roll(a, k, 0), 1.0)
            b = b + a * b_sh
            a = a * a_sh
            k *= 2
        h = b + a * h_carry[SUBLANE - 1:SUBLANE, lanes]
        h_carry[:, lanes] = h[tt - SUBLANE:tt, :]
        hlast_ref[0, :, lanes] = h[tt - SUBLANE:tt, :]
        gel = _gelu_tanh(z_ref[:, c_gel + n * LANE:c_gel + (n + 1) * LANE])
        f_ref[:, pool_w + n * LANE:pool_w + (n + 1) * LANE] = (h * gel).astype(BF16)
    lru_ext[0:LRU_HALO, :] = lru_ext[tt:tt + LRU_HALO, :]

    conf_ext[CONF_HALO:CONF_HALO + tt, :] = (
        z_ref[:, c_ca:c_ca + conf_w] * _sigmoid(z_ref[:, c_cb:c_cb + conf_w]))
    for c in range(conf_w // LANE):
        lanes = slice(c * LANE, (c + 1) * LANE)
        acc = bcc_ref[:, lanes] + wcc_ref[conf_taps - 1:conf_taps, lanes] * conf_ext[CONF_HALO:CONF_HALO + tt, lanes]
        for k in range(conf_taps - 1):
            off = CONF_HALO - (conf_taps - 1) + k
            acc = acc + wcc_ref[k:k + 1, lanes] * conf_ext[off:off + tt, lanes]
        vc_buf[:, lanes] = acc
    for r in range(tt // LN_ROWS):
        rows = slice(r * LN_ROWS, (r + 1) * LN_ROWS)
        y = _layer_norm_silu(vc_buf[rows, :], gcf_ref[...], bcf_ref[...])
        f_ref[rows, pool_w + lru_w:pool_w + lru_w + conf_w] = y.astype(BF16)
    vlast_ref[0] = conf_ext[tt:tt + CONF_HALO, :]
    conf_ext[0:CONF_HALO, :] = conf_ext[tt:tt + CONF_HALO, :]


def _mixer_prompt(zg, wgrp, pscale, wlc, blc, wax, ba, bx, lam, wcc, bcc, gcf, bcf,
                  *, batch, seq, tt, pool_w, lru_w, conf_w):
    nt = seq // tt
    zc = pool_w + 2 * lru_w + 2 * conf_w
    fw = pool_w + lru_w + conf_w
    lru_taps = wlc.shape[0]
    conf_taps = wcc.shape[0]
    const2 = lambda b, t: (0, 0)
    const3 = lambda b, t: (0, 0, 0)
    vmem = (2 * tt * zc * 4 + 2 * tt * fw * 2
            + (POOL_HALO + tt) * pool_w * 4 + (LRU_HALO + tt) * lru_w * 4
            + (CONF_HALO + tt) * conf_w * 4 + tt * conf_w * 4 + (8 << 20))
    kern = functools.partial(_mixer_prompt_kernel, tt=tt, pool_w=pool_w, lru_w=lru_w,
                             conf_w=conf_w, lru_taps=lru_taps, conf_taps=conf_taps)
    return pl.pallas_call(
        kern,
        grid=(batch, nt),
        in_specs=[
            pl.BlockSpec((tt, zc), lambda b, t: (b * nt + t, 0)),
            pl.BlockSpec(wgrp.shape, const3),
            pl.BlockSpec(pscale.shape, const2),
            pl.BlockSpec(wlc.shape, const2),
            pl.BlockSpec(blc.shape, const2),
            pl.BlockSpec(wax.shape, const3),
            pl.BlockSpec(ba.shape, const2),
            pl.BlockSpec(bx.shape, const2),
            pl.BlockSpec(lam.shape, const2),
            pl.BlockSpec(wcc.shape, const2),
            pl.BlockSpec(bcc.shape, const2),
            pl.BlockSpec(gcf.shape, const2),
            pl.BlockSpec(bcf.shape, const2),
        ],
        out_specs=[
            pl.BlockSpec((tt, fw), lambda b, t: (b * nt + t, 0)),
            pl.BlockSpec((1, SUBLANE, lru_w), lambda b, t: (b, 0, 0)),
            pl.BlockSpec((1, CONF_HALO, conf_w), lambda b, t: (b, 0, 0)),
        ],
        out_shape=[
            jax.ShapeDtypeStruct((batch * seq, fw), BF16),
            jax.ShapeDtypeStruct((batch, SUBLANE, lru_w), F32),
            jax.ShapeDtypeStruct((batch, CONF_HALO, conf_w), F32),
        ],
        scratch_shapes=[
            pltpu.VMEM((POOL_HALO + tt, pool_w), F32),
            pltpu.VMEM((LRU_HALO + tt, lru_w), F32),
            pltpu.VMEM((CONF_HALO + tt, conf_w), F32),
            pltpu.VMEM((tt, conf_w), F32),
            pltpu.VMEM((SUBLANE, lru_w), F32),
        ],
        compiler_params=_params(vmem, 2),
        name="mixer_prompt",
    )(zg, wgrp, pscale, wlc, blc, wax, ba, bx, lam, wcc, bcc, gcf, bcf)


def _mixer_sample_kernel(z_ref, spool_ref, slru_ref, sh_ref, sconf_ref,
                         wgrp_ref, pscale_ref, wlc_ref, blc_ref, wax_ref, ba_ref,
                         bx_ref, lam_ref, wcc_ref, bcc_ref, gcf_ref, bcf_ref,
                         f_ref, hnew_ref, v_ref, vc_buf,
                         *, pool_w, lru_w, conf_w, lru_taps, conf_taps, pool_buf, start_pos):
    c_lru = pool_w
    c_gel = c_lru + lru_w
    c_ca = c_gel + lru_w
    c_cb = c_ca + conf_w

    for g, w in enumerate(POOL_WINDOWS):
        lanes = slice(g * LANE, (g + 1) * LANE)
        u = z_ref[:, lanes]
        s = u
        for j in range(1, w):
            trow = pool_buf - j
            s = s + spool_ref[:, trow * pool_w + g * LANE:trow * pool_w + (g + 1) * LANE]
        cnt = float(min(w, start_pos + 1))
        d = s / cnt - u
        y = jnp.dot(d.astype(BF16), wgrp_ref[g], preferred_element_type=F32)
        f_ref[:, lanes] = (y * pscale_ref[:, lanes]).astype(BF16)

    sp_all = _softplus(-lam_ref[...])
    for n in range(lru_w // LANE):
        lanes = slice(n * LANE, (n + 1) * LANE)
        xc = blc_ref[:, lanes] + wlc_ref[lru_taps - 1:lru_taps, lanes] * z_ref[:, c_lru + n * LANE:c_lru + (n + 1) * LANE]
        for k in range(lru_taps - 1):
            xc = xc + wlc_ref[k:k + 1, lanes] * slru_ref[:, k * lru_w + n * LANE:k * lru_w + (n + 1) * LANE]
        ri = jnp.dot(xc.astype(BF16), wax_ref[n], preferred_element_type=F32)
        a, gate_i, mult = _lru_coeffs(xc, ri, ba_ref[:, lanes], bx_ref[:, lanes], sp_all[:, lanes])
        if start_pos == 0:
            a = jnp.zeros_like(a)
            mult = jnp.ones_like(mult)
        h = a * sh_ref[:, lanes] + xc * gate_i * mult
        hnew_ref[:, lanes] = h
        gel = _gelu_tanh(z_ref[:, c_gel + n * LANE:c_gel + (n + 1) * LANE])
        f_ref[:, pool_w + n * LANE:pool_w + (n + 1) * LANE] = (h * gel).astype(BF16)

    for c in range(conf_w // LANE):
        lanes = slice(c * LANE, (c + 1) * LANE)
        v = z_ref[:, c_ca + c * LANE:c_ca + (c + 1) * LANE] * _sigmoid(
            z_ref[:, c_cb + c * LANE:c_cb + (c + 1) * LANE])
        v_ref[:, lanes] = v
        acc = bcc_ref[:, lanes] + wcc_ref[conf_taps - 1:conf_taps, lanes] * v
        for k in range(conf_taps - 1):
            acc = acc + wcc_ref[k:k + 1, lanes] * sconf_ref[:, k * conf_w + c * LANE:k * conf_w + (c + 1) * LANE]
        vc_buf[:, lanes] = acc
    y = _layer_norm_silu(vc_buf[...], gcf_ref[...], bcf_ref[...])
    f_ref[:, pool_w + lru_w:pool_w + lru_w + conf_w] = y.astype(BF16)


def _mixer_sample(zg, spool, slru, sh, sconf, wgrp, pscale, wlc, blc, wax, ba, bx, lam,
                  wcc, bcc, gcf, bcf, *, row0, pool_w, lru_w, conf_w, start_pos):
    nb = sh.shape[0]
    zc = pool_w + 2 * lru_w + 2 * conf_w
    fw = pool_w + lru_w + conf_w
    lru_taps = wlc.shape[0]
    conf_taps = wcc.shape[0]
    pool_buf = spool.shape[1] // pool_w
    const2 = lambda i: (0, 0)
    const3 = lambda i: (0, 0, 0)
    ins = (spool, slru, sh, sconf, wgrp, pscale, wlc, blc, wax, ba, bx, lam, wcc, bcc, gcf, bcf)
    vmem = 2 * (nb * zc * 4 + sum(a.size * a.dtype.itemsize for a in ins)) + (8 << 20)
    kern = functools.partial(_mixer_sample_kernel, pool_w=pool_w, lru_w=lru_w, conf_w=conf_w,
                             lru_taps=lru_taps, conf_taps=conf_taps, pool_buf=pool_buf,
                             start_pos=start_pos)
    return pl.pallas_call(
        kern,
        grid=(1,),
        in_specs=[pl.BlockSpec((nb, zc), lambda i: (row0 // nb, 0))] + [
            pl.BlockSpec(a.shape, const3 if a.ndim == 3 else const2) for a in ins],
        out_specs=[
            pl.BlockSpec((nb, fw), const2),
            pl.BlockSpec((nb, lru_w), const2),
            pl.BlockSpec((nb, conf_w), const2),
        ],
        out_shape=[
            jax.ShapeDtypeStruct((nb, fw), BF16),
            jax.ShapeDtypeStruct((nb, lru_w), F32),
            jax.ShapeDtypeStruct((nb, conf_w), F32),
        ],
        scratch_shapes=[pltpu.VMEM((nb, conf_w), F32)],
        compiler_params=_params(vmem, 1),
        name="mixer_sample",
    )(zg, *ins)


def _merge_kernel(f_ref, ga_ref, gb_ref, gc_ref, wbr_ref, wout_ref, x_ref, o_ref,
                  *, pool_w, lru_w):
    n = pl.program_id(1)
    c1 = pool_w
    c2 = pool_w + lru_w
    yp = jnp.dot(f_ref[:, 0:c1], wbr_ref[0:c1, :], preferred_element_type=F32)
    yl = jnp.dot(f_ref[:, c1:c2], wbr_ref[c1:c2, :], preferred_element_type=F32)
    yc = jnp.dot(f_ref[:, c2:], wbr_ref[c2:, :], preferred_element_type=F32)
    merged = (ga_ref[...] * yp + gb_ref[...] * yl + gc_ref[...] * yc).astype(BF16)
    contrib = jnp.dot(merged, wout_ref[...], preferred_element_type=F32)

    @pl.when(n == 0)
    def _():
        o_ref[...] = x_ref[...] + contrib

    @pl.when(n > 0)
    def _():
        o_ref[...] += contrib


def _merge(f, zg, wbr, wout, x, *, gate_col0, pool_w, lru_w, tm, tn):
    m, d = x.shape
    fw = f.shape[1]
    g0 = gate_col0 // tn
    nd = d // tn
    vmem = (2 * tm * fw * 2 + 3 * 2 * tm * tn * 4 + 2 * fw * tn * 2 + 2 * tn * d * 2
            + 4 * tm * d * 4 + 3 * tm * tn * 4)
    return pl.pallas_call(
        functools.partial(_merge_kernel, pool_w=pool_w, lru_w=lru_w),
        grid=(m // tm, nd),
        in_specs=[
            pl.BlockSpec((tm, fw), lambda i, j: (i, 0)),
            pl.BlockSpec((tm, tn), lambda i, j: (i, g0 + j)),
            pl.BlockSpec((tm, tn), lambda i, j: (i, g0 + nd + j)),
            pl.BlockSpec((tm, tn), lambda i, j: (i, g0 + 2 * nd + j)),
            pl.BlockSpec((fw, tn), lambda i, j: (0, j)),
            pl.BlockSpec((tn, d), lambda i, j: (j, 0)),
            pl.BlockSpec((tm, d), lambda i, j: (i, 0)),
        ],
        out_specs=pl.BlockSpec((tm, d), lambda i, j: (i, 0)),
        out_shape=jax.ShapeDtypeStruct((m, d), F32),
        compiler_params=_params(vmem, 2),
        name="merge",
    )(f, zg, zg, zg, wbr, wout, x)


def _mlp_kernel(x_ref, g_ref, wup_ref, wdn_ref, o_ref, xn_ref):
    f = pl.program_id(1)

    @pl.when(f == 0)
    def _():
        x = x_ref[...]
        ms = jnp.mean(x * x, axis=-1, keepdims=True)
        xn_ref[...] = ((x * lax.rsqrt(ms + EPS)) * g_ref[...]).astype(BF16)

    hid = jnp.dot(xn_ref[...], wup_ref[...], preferred_element_type=F32)
    hid = jnp.maximum(hid, 0.0)
    hid = (hid * hid).astype(BF16)
    contrib = jnp.dot(hid, wdn_ref[...], preferred_element_type=F32)

    @pl.when(f == 0)
    def _():
        o_ref[...] = x_ref[...] + contrib

    @pl.when(f > 0)
    def _():
        o_ref[...] += contrib


def _mlp(x, g, wup, wdn, *, tm, tf):
    m, d = x.shape
    dff = wup.shape[1]
    vmem = 4 * tm * d * 4 + tm * d * 2 + 2 * d * tf * 2 + 2 * tf * d * 2 + 2 * tm * tf * 4 + tm * d * 4
    return pl.pallas_call(
        _mlp_kernel,
        grid=(m // tm, dff // tf),
        in_specs=[
            pl.BlockSpec((tm, d), lambda i, j: (i, 0)),
            pl.BlockSpec((1, d), lambda i, j: (0, 0)),
            pl.BlockSpec((d, tf), lambda i, j: (0, j)),
            pl.BlockSpec((tf, d), lambda i, j: (j, 0)),
        ],
        out_specs=pl.BlockSpec((tm, d), lambda i, j: (i, 0)),
        out_shape=jax.ShapeDtypeStruct((m, d), F32),
        scratch_shapes=[pltpu.VMEM((tm, d), BF16)],
        compiler_params=_params(vmem, 2),
        name="mlp",
    )(x, g, wup, wdn)


def _final_norm_kernel(x_ref, g_ref, o_ref):
    x = x_ref[...]
    ms = jnp.mean(x * x, axis=-1, keepdims=True)
    o_ref[...] = (x * lax.rsqrt(ms + EPS)) * g_ref[...]


def _final_norm(x, g, *, tm):
    m, d = x.shape
    return pl.pallas_call(
        _final_norm_kernel,
        grid=(m // tm,),
        in_specs=[pl.BlockSpec((tm, d), lambda i: (i, 0)),
                  pl.BlockSpec((1, d), lambda i: (0, 0))],
        out_specs=pl.BlockSpec((tm, d), lambda i: (i, 0)),
        out_shape=jax.ShapeDtypeStruct((m, d), F32),
        compiler_params=_params(4 * tm * d * 4, 1),
        name="final_norm",
    )(x, g)


def kernel(x_prompt, x_sample, state_pool, state_lru_conv, state_lru_h, state_conf_conv, g_mix, w_in, w_pool_grp, pool_scale, w_pool_br, w_lru_conv, b_lru_conv, w_lru_a, b_lru_a, w_lru_x, b_lru_x, lru_lambda, w_lru_br, w_conf_conv, b_conf_conv, g_conf, b_conf, w_conf_br, w_gate, b_gate, w_out, g_mlp, w_up, w_down, g_final):
    batch, seq, d = x_prompt.shape
    nb, dec_seq, _ = x_sample.shape
    assert dec_seq == 1
    depth = w_in.shape[0]
    pool_w = w_pool_br.shape[1]
    lru_w = w_lru_br.shape[1]
    conf_w = w_conf_br.shape[1]
    in_cols = w_in.shape[2]
    pool_buf = state_pool.shape[2]
    lru_buf = state_lru_conv.shape[2]
    conf_buf = state_conf_conv.shape[2]
    n_prompt = batch * seq
    m = n_prompt + nb
    start_pos = PAST_LEN

    tm = m // 8
    tm_merge = m // 13
    tn = 512
    tt = 256

    x = jnp.concatenate([x_prompt.reshape(n_prompt, d), x_sample.reshape(nb, d)], axis=0)

    w1 = jnp.concatenate([w_in, w_gate], axis=2).astype(BF16)
    b1 = jnp.concatenate([jnp.zeros((depth, in_cols), F32), b_gate], axis=1)[:, None, :]
    wbr = jnp.concatenate([w_pool_br, w_lru_br, w_conf_br], axis=1).astype(BF16)
    wout = w_out.astype(BF16)
    wup = w_up.astype(BF16)
    wdn = w_down.astype(BF16)
    wgrp = w_pool_grp.astype(BF16)
    wax = jnp.concatenate([w_lru_a, w_lru_x], axis=3).astype(BF16)
    row = lambda a: a[:, None, :]
    g_mix_r, g_mlp_r = row(g_mix), row(g_mlp)
    pscale_r, blc_r, ba_r, bx_r, lam_r = row(pool_scale), row(b_lru_conv), row(b_lru_a), row(b_lru_x), row(lru_lambda)
    bcc_r, gcf_r, bcf_r = row(b_conf_conv), row(g_conf), row(b_conf)

    spool2 = state_pool.reshape(depth, nb, pool_buf * pool_w)
    slru2 = state_lru_conv.reshape(depth, nb, lru_buf * lru_w)
    sconf2 = state_conf_conv.reshape(depth, nb, conf_buf * conf_w)

    outs = [[] for _ in range(8)]
    for l in range(depth):
        zg = _inproj(x, g_mix_r[l], w1[l], b1[l], n_plain_cols=in_cols, tm=tm, tn=tn)
        mix_w = (wgrp[l], pscale_r[l], w_lru_conv[l], blc_r[l], wax[l], ba_r[l], bx_r[l],
                 lam_r[l], w_conf_conv[l], bcc_r[l], gcf_r[l], bcf_r[l])
        f_p, h8_p, v_p = _mixer_prompt(zg, *mix_w, batch=batch, seq=seq, tt=tt,
                                       pool_w=pool_w, lru_w=lru_w, conf_w=conf_w)
        f_s, h_s, v_s = _mixer_sample(zg, spool2[l], slru2[l], state_lru_h[l], sconf2[l], *mix_w,
                                      row0=n_prompt, pool_w=pool_w, lru_w=lru_w, conf_w=conf_w,
                                      start_pos=start_pos)
        f = jnp.concatenate([f_p, f_s], axis=0)
        x = _merge(f, zg, wbr[l], wout[l], x, gate_col0=in_cols, pool_w=pool_w, lru_w=lru_w,
                   tm=tm_merge, tn=tn)
        x = _mlp(x, g_mlp_r[l], wup[l], wdn[l], tm=tm, tf=tn)

        zp = zg[:n_prompt].reshape(batch, seq, -1)
        zs = zg[n_prompt:]
        outs[0].append(zp[:, seq - pool_buf:, 0:pool_w])
        outs[1].append(zp[:, seq - lru_buf:, pool_w:pool_w + lru_w])
        outs[2].append(h8_p[:, SUBLANE - 1, :])
        outs[3].append(v_p[:, CONF_HALO - conf_buf:, :])
        outs[4].append(jnp.concatenate([state_pool[l][:, 1:], zs[:, None, 0:pool_w]], axis=1))
        outs[5].append(jnp.concatenate([state_lru_conv[l][:, 1:], zs[:, None, pool_w:pool_w + lru_w]], axis=1))
        outs[6].append(h_s)
        outs[7].append(jnp.concatenate([state_conf_conv[l][:, 1:], v_s[:, None, :]], axis=1))

    y = _final_norm(x, g_final[None, :], tm=tm)
    y_prompt = y[:n_prompt].reshape(batch, seq, d)
    y_sample = y[n_prompt:].reshape(nb, 1, d)
    return (y_prompt, y_sample) + tuple(jnp.stack(o) for o in outs)
```

```python
import functools
import math

import jax
import jax.numpy as jnp
from jax import lax
from jax.experimental import pallas as pl
from jax.experimental.pallas import tpu as pltpu

F32 = jnp.float32
BF16 = jnp.bfloat16

POOL_WINDOWS = (2, 4, 8, 16)
LRU_C = 8.0
EPS = 1e-6
PAST_LEN = 16384
LANE = 128
SUBLANE = 8
NT = 8
VMEM_SLACK = 8 << 20
VMEM_CAP = 60 << 20


def _sigmoid(x):
    return 0.5 * jnp.tanh(0.5 * x) + 0.5


def _gelu_tanh(x):
    c = math.sqrt(2.0 / math.pi)
    return x * (0.5 * (1.0 + jnp.tanh(c * (x + 0.044715 * (x * x * x)))))


def _softplus(x):
    return jnp.maximum(x, 0.0) + jnp.log1p(jnp.exp(-jnp.abs(x)))


def _rms_norm(x, g):
    ms = jnp.mean(x * x, axis=-1, keepdims=True)
    return (x * lax.rsqrt(ms + EPS)) * g


def _params(vmem_bytes, n_axes):
    return pltpu.CompilerParams(
        dimension_semantics=("arbitrary",) * n_axes,
        vmem_limit_bytes=int(min(vmem_bytes + VMEM_SLACK, VMEM_CAP)))


def _entry_kernel(xp_ref, xs_ref, o_ref):
    rp = xp_ref.shape[0]
    o_ref[0:rp, :] = xp_ref[...]
    o_ref[rp:, :] = xs_ref[...]


def _entry(xp, xs):
    d = xp.shape[1]
    rp, rs = xp.shape[0] // NT, xs.shape[0] // NT
    return pl.pallas_call(
        _entry_kernel,
        grid=(NT,),
        in_specs=[pl.BlockSpec((rp, d), lambda i: (i, 0)),
                  pl.BlockSpec((rs, d), lambda i: (i, 0))],
        out_specs=pl.BlockSpec((None, rp + rs, d), lambda i: (i, 0, 0)),
        out_shape=jax.ShapeDtypeStruct((NT, rp + rs, d), F32),
        compiler_params=_params(4 * (rp + rs) * d * 4, 1),
        name="entry",
    )(xp, xs)


def _inproj_kernel(x_ref, g_ref, w_ref, z_ref, xn_ref):
    @pl.when(pl.program_id(1) == 0)
    def _():
        xn_ref[...] = _rms_norm(x_ref[...], g_ref[...]).astype(BF16)

    z_ref[...] = jnp.dot(xn_ref[...], w_ref[...].astype(BF16), preferred_element_type=F32)


def _inproj(x, g, w, *, tn):
    nt, tm, d = x.shape
    ncols = w.shape[1]
    vmem = tm * d * 4 + 2 * tm * d * 2 + 2 * d * tn * 4 + d * tn * 2 + 2 * tm * tn * 4
    return pl.pallas_call(
        _inproj_kernel,
        grid=(nt, ncols // tn),
        in_specs=[
            pl.BlockSpec((None, tm, d), lambda i, j: (i, 0, 0), pipeline_mode=pl.Buffered(1)),
            pl.BlockSpec((1, d), lambda i, j: (0, 0)),
            pl.BlockSpec((d, tn), lambda i, j: (0, j)),
        ],
        out_specs=[
            pl.BlockSpec((None, tm, tn), lambda i, j: (i, 0, j)),
            pl.BlockSpec((None, tm, d), lambda i, j: (i, 0, 0)),
        ],
        out_shape=[
            jax.ShapeDtypeStruct((nt, tm, ncols), F32),
            jax.ShapeDtypeStruct((nt, tm, d), BF16),
        ],
        compiler_params=_params(vmem, 2),
        name="inproj",
    )(x, g, w)


def _lru_coeffs(ri, ba, bx, sp):
    gate_r = _sigmoid(ri[:, :LANE] + ba)
    gate_i = _sigmoid(ri[:, LANE:] + bx)
    log_a = (-LRU_C) * gate_r * sp
    a = jnp.exp(log_a)
    th = jnp.tanh(log_a)
    mult = jnp.sqrt((-2.0 * th) / (1.0 - th))
    return a, gate_i, mult


def _layer_norm_silu(vc, g, b):
    mu = jnp.mean(vc, axis=-1, keepdims=True)
    cen = vc - mu
    var = jnp.mean(cen * cen, axis=-1, keepdims=True)
    y = (cen * lax.rsqrt(var + EPS)) * g + b
    return y * _sigmoid(y)


def _scan_tile(a, b, carry, tt):
    groups = tt // SUBLANE
    a3 = a.reshape(groups, SUBLANE, LANE)
    b3 = b.reshape(groups, SUBLANE, LANE)
    sub = lax.broadcasted_iota(jnp.int32, (groups, SUBLANE, LANE), 1)
    for k in (1, 2, 4):
        keep = sub >= k
        b_sh = jnp.where(keep, pltpu.roll(b3, k, 1), 0.0)
        a_sh = jnp.where(keep, pltpu.roll(a3, k, 1), 1.0)
        b3 = b3 + a3 * b_sh
        a3 = a3 * a_sh
    hs = []
    for j in range(groups):
        hj = b3[j] + a3[j] * carry
        hs.append(hj)
        carry = hj[SUBLANE - 1:SUBLANE, :]
    return jnp.concatenate(hs, axis=0)


POOL_HALO = 16
LRU_HALO = 8
CONF_HALO = 32
LN_ROWS = 64


def _mixer_prompt_kernel(z_ref, wgrp_ref, pscale_ref, wlc_ref, blc_ref, wa_ref, wx_ref, ba_ref,
                         bx_ref, lam_ref, wcc_ref, bcc_ref, gcf_ref, bcf_ref,
                         f_ref, plast_ref, llast_ref, hlast_ref, vlast_ref,
                         pool_ext, lru_ext, conf_ext, vc_buf, h_carry,
                         *, tt, pool_w, lru_w, conf_w, lru_taps, conf_taps):
    t = pl.program_id(1)
    c_lru = pool_w
    c_gel = c_lru + lru_w
    c_ca = c_gel + lru_w
    c_cb = c_ca + conf_w

    @pl.when(t == 0)
    def _():
        pool_ext[0:POOL_HALO, :] = jnp.zeros((POOL_HALO, pool_w), F32)
        lru_ext[0:LRU_HALO, :] = jnp.zeros((LRU_HALO, lru_w), F32)
        conf_ext[0:CONF_HALO, :] = jnp.zeros((CONF_HALO, conf_w), F32)
        h_carry[...] = jnp.zeros_like(h_carry)

    row = lax.broadcasted_iota(jnp.int32, (tt, LANE), 0)
    pos = row + t * tt
    posf = pos.astype(F32)
    is_first = pos == 0

    pool_ext[POOL_HALO:POOL_HALO + tt, :] = z_ref[:, 0:pool_w]
    for g, w in enumerate(POOL_WINDOWS):
        lanes = slice(g * LANE, (g + 1) * LANE)
        u = pool_ext[POOL_HALO:POOL_HALO + tt, lanes]
        s = u
        for j in range(1, w):
            s = s + pool_ext[POOL_HALO - j:POOL_HALO - j + tt, lanes]
        cnt = jnp.minimum(jnp.float32(w), posf + 1.0)
        d = s / cnt - u
        y = jnp.dot(d.astype(BF16), wgrp_ref[g].astype(BF16), preferred_element_type=F32)
        f_ref[:, lanes] = (y * pscale_ref[:, lanes]).astype(BF16)
    plast_ref[...] = pool_ext[tt:tt + POOL_HALO, :]
    pool_ext[0:POOL_HALO, :] = pool_ext[tt:tt + POOL_HALO, :]

    lru_ext[LRU_HALO:LRU_HALO + tt, :] = z_ref[:, c_lru:c_lru + lru_w]
    sp_all = _softplus(-lam_ref[...])
    for n in range(lru_w // LANE):
        lanes = slice(n * LANE, (n + 1) * LANE)
        xc = blc_ref[:, lanes] + wlc_ref[lru_taps - 1:lru_taps, lanes] * lru_ext[LRU_HALO:LRU_HALO + tt, lanes]
        for k in range(lru_taps - 1):
            off = LRU_HALO - (lru_taps - 1) + k
            xc = xc + wlc_ref[k:k + 1, lanes] * lru_ext[off:off + tt, lanes]
        xcb = xc.astype(BF16)
        ri = jnp.concatenate(
            [jnp.dot(xcb, wa_ref[n].astype(BF16), preferred_element_type=F32),
             jnp.dot(xcb, wx_ref[n].astype(BF16), preferred_element_type=F32)], axis=1)
        a, gate_i, mult = _lru_coeffs(ri, ba_ref[:, lanes], bx_ref[:, lanes], sp_all[:, lanes])
        a = jnp.where(is_first, 0.0, a)
        mult = jnp.where(is_first, 1.0, mult)
        b = xc * gate_i * mult
        h = _scan_tile(a, b, h_carry[SUBLANE - 1:SUBLANE, lanes], tt)
        h_carry[:, lanes] = h[tt - SUBLANE:tt, :]
        hlast_ref[:, lanes] = h[tt - SUBLANE:tt, :]
        gel = _gelu_tanh(z_ref[:, c_gel + n * LANE:c_gel + (n + 1) * LANE])
        f_ref[:, pool_w + n * LANE:pool_w + (n + 1) * LANE] = (h * gel).astype(BF16)
    llast_ref[...] = lru_ext[tt:tt + LRU_HALO, :]
    lru_ext[0:LRU_HALO, :] = lru_ext[tt:tt + LRU_HALO, :]

    conf_ext[CONF_HALO:CONF_HALO + tt, :] = (
        z_ref[:, c_ca:c_ca + conf_w] * _sigmoid(z_ref[:, c_cb:c_cb + conf_w]))
    base = CONF_HALO - (conf_taps - 1)
    for c in range(conf_w // LANE):
        lanes = slice(c * LANE, (c + 1) * LANE)
        acc = None
        for r in range(SUBLANE):
            offs = [o for o in range(base, CONF_HALO + 1) if o % SUBLANE == r]
            if not offs:
                continue
            part = None
            for o in offs:
                k = o - base
                term = wcc_ref[k:k + 1, lanes] * conf_ext[o:o + tt, lanes]
                part = term if part is None else part + term
            acc = part if acc is None else acc + part
        vc_buf[:, lanes] = acc + bcc_ref[:, lanes]
    for r in range(tt // LN_ROWS):
        rows = slice(r * LN_ROWS, (r + 1) * LN_ROWS)
        y = _layer_norm_silu(vc_buf[rows, :], gcf_ref[...], bcf_ref[...])
        f_ref[rows, pool_w + lru_w:pool_w + lru_w + conf_w] = y.astype(BF16)
    vlast_ref[...] = conf_ext[tt:tt + CONF_HALO, :]
    conf_ext[0:CONF_HALO, :] = conf_ext[tt:tt + CONF_HALO, :]


def _mixer_prompt(z, wgrp, pscale, wlc, blc, wa, wx, ba, bx, lam, wcc, bcc, gcf, bcf,
                  *, batch, seq, rows_p, tt, pool_w, lru_w, conf_w):
    nt, tm, zc = z.shape
    tiles_per_seq = seq // rows_p
    steps_per_tile = rows_p // tt
    fw = pool_w + lru_w + conf_w
    lru_taps = wlc.shape[0]
    conf_taps = wcc.shape[0]
    const2 = lambda b, t: (0, 0)
    const3 = lambda b, t: (0, 0, 0)
    zmap = lambda b, t: (b * tiles_per_seq + t // steps_per_tile, t % steps_per_tile, 0)
    smap = lambda b, t: (b, 0, 0)
    vmem = (2 * tt * zc * 4 + 2 * tt * fw * 2
            + (POOL_HALO + tt) * pool_w * 4 + (LRU_HALO + tt) * lru_w * 4
            + (CONF_HALO + tt) * conf_w * 4 + tt * conf_w * 4 + (8 << 20))
    kern = functools.partial(_mixer_prompt_kernel, tt=tt, pool_w=pool_w, lru_w=lru_w,
                             conf_w=conf_w, lru_taps=lru_taps, conf_taps=conf_taps)
    weights = (wgrp, pscale, wlc, blc, wa, wx, ba, bx, lam, wcc, bcc, gcf, bcf)
    return pl.pallas_call(
        kern,
        grid=(batch, seq // tt),
        in_specs=[pl.BlockSpec((None, tt, zc), zmap)] + [
            pl.BlockSpec(a.shape, const3 if a.ndim == 3 else const2) for a in weights],
        out_specs=[
            pl.BlockSpec((None, tt, fw), zmap),
            pl.BlockSpec((None, POOL_HALO, pool_w), smap),
            pl.BlockSpec((None, LRU_HALO, lru_w), smap),
            pl.BlockSpec((None, SUBLANE, lru_w), smap),
            pl.BlockSpec((None, CONF_HALO, conf_w), smap),
        ],
        out_shape=[
            jax.ShapeDtypeStruct((nt, tm, fw), BF16),
            jax.ShapeDtypeStruct((batch, POOL_HALO, pool_w), F32),
            jax.ShapeDtypeStruct((batch, LRU_HALO, lru_w), F32),
            jax.ShapeDtypeStruct((batch, SUBLANE, lru_w), F32),
            jax.ShapeDtypeStruct((batch, CONF_HALO, conf_w), F32),
        ],
        scratch_shapes=[
            pltpu.VMEM((POOL_HALO + tt, pool_w), F32),
            pltpu.VMEM((LRU_HALO + tt, lru_w), F32),
            pltpu.VMEM((CONF_HALO + tt, conf_w), F32),
            pltpu.VMEM((tt, conf_w), F32),
            pltpu.VMEM((SUBLANE, lru_w), F32),
        ],
        compiler_params=_params(vmem, 2),
        name="mixer_prompt",
    )(z, *weights)


def _mixer_sample_kernel(f_any, z_ref, spool_ref, slru_ref, sh_ref, sconf_ref,
                         wgrp_ref, pscale_ref, wlc_ref, blc_ref, wa_ref, wx_ref, ba_ref,
                         bx_ref, lam_ref, wcc_ref, bcc_ref, gcf_ref, bcf_ref,
                         f_ref, npool_ref, nlru_ref, hnew_ref, nconf_ref, vc_buf,
                         *, pool_w, lru_w, conf_w, lru_taps, conf_taps, pool_buf, start_pos):
    del f_any
    c_lru = pool_w
    c_gel = c_lru + lru_w
    c_ca = c_gel + lru_w
    c_cb = c_ca + conf_w

    for g, w in enumerate(POOL_WINDOWS):
        lanes = slice(g * LANE, (g + 1) * LANE)
        u = z_ref[:, lanes]
        s = u
        for j in range(1, w):
            trow = pool_buf - j
            s = s + spool_ref[:, trow * pool_w + g * LANE:trow * pool_w + (g + 1) * LANE]
        cnt = float(min(w, start_pos + 1))
        d = s / cnt - u
        y = jnp.dot(d.astype(BF16), wgrp_ref[g].astype(BF16), preferred_element_type=F32)
        f_ref[:, lanes] = (y * pscale_ref[:, lanes]).astype(BF16)
    npool_ref[:, 0:(pool_buf - 1) * pool_w] = spool_ref[:, pool_w:]
    npool_ref[:, (pool_buf - 1) * pool_w:] = z_ref[:, 0:pool_w]

    sp_all = _softplus(-lam_ref[...])
    for n in range(lru_w // LANE):
        lanes = slice(n * LANE, (n + 1) * LANE)
        xc = blc_ref[:, lanes] + wlc_ref[lru_taps - 1:lru_taps, lanes] * z_ref[:, c_lru + n * LANE:c_lru + (n + 1) * LANE]
        for k in range(lru_taps - 1):
            xc = xc + wlc_ref[k:k + 1, lanes] * slru_ref[:, k * lru_w + n * LANE:k * lru_w + (n + 1) * LANE]
        xcb = xc.astype(BF16)
        ri = jnp.concatenate(
            [jnp.dot(xcb, wa_ref[n].astype(BF16), preferred_element_type=F32),
             jnp.dot(xcb, wx_ref[n].astype(BF16), preferred_element_type=F32)], axis=1)
        a, gate_i, mult = _lru_coeffs(ri, ba_ref[:, lanes], bx_ref[:, lanes], sp_all[:, lanes])
        if start_pos == 0:
            a = jnp.zeros_like(a)
            mult = jnp.ones_like(mult)
        h = a * sh_ref[:, lanes] + xc * gate_i * mult
        hnew_ref[:, lanes] = h
        gel = _gelu_tanh(z_ref[:, c_gel + n * LANE:c_gel + (n + 1) * LANE])
        f_ref[:, pool_w + n * LANE:pool_w + (n + 1) * LANE] = (h * gel).astype(BF16)
    nlru_ref[:, 0:(lru_taps - 2) * lru_w] = slru_ref[:, lru_w:]
    nlru_ref[:, (lru_taps - 2) * lru_w:] = z_ref[:, c_lru:c_lru + lru_w]

    for c in range(conf_w // LANE):
        lanes = slice(c * LANE, (c + 1) * LANE)
        v = z_ref[:, c_ca + c * LANE:c_ca + (c + 1) * LANE] * _sigmoid(
            z_ref[:, c_cb + c * LANE:c_cb + (c + 1) * LANE])
        nconf_ref[:, (conf_taps - 2) * conf_w + c * LANE:(conf_taps - 2) * conf_w + (c + 1) * LANE] = v
        acc = bcc_ref[:, lanes] + wcc_ref[conf_taps - 1:conf_taps, lanes] * v
        for k in range(conf_taps - 1):
            acc = acc + wcc_ref[k:k + 1, lanes] * sconf_ref[:, k * conf_w + c * LANE:k * conf_w + (c + 1) * LANE]
        vc_buf[:, lanes] = acc
    nconf_ref[:, 0:(conf_taps - 2) * conf_w] = sconf_ref[:, conf_w:]
    y = _layer_norm_silu(vc_buf[...], gcf_ref[...], bcf_ref[...])
    f_ref[:, pool_w + lru_w:pool_w + lru_w + conf_w] = y.astype(BF16)


def _mixer_sample(f, z, spool, slru, sh, sconf, wgrp, pscale, wlc, blc, wa, wx, ba, bx, lam,
                  wcc, bcc, gcf, bcf, *, rows_p, pool_w, lru_w, conf_w, start_pos):
    nt, tm, zc = z.shape
    rs = tm - rows_p
    nb = sh.shape[0]
    fw = pool_w + lru_w + conf_w
    lru_taps = wlc.shape[0]
    conf_taps = wcc.shape[0]
    pool_buf = spool.shape[1] // pool_w
    const2 = lambda i: (0, 0)
    const3 = lambda i: (0, 0, 0)
    rowmap = lambda i: (i, rows_p // rs, 0)
    bmap = lambda i: (i, 0)
    states = (spool, slru, sh, sconf)
    weights = (wgrp, pscale, wlc, blc, wa, wx, ba, bx, lam, wcc, bcc, gcf, bcf)
    vmem = (4 * rs * sum(s.shape[1] for s in states) * 4 * 2
            + 2 * sum(a.size * a.dtype.itemsize for a in weights) + (8 << 20))
    kern = functools.partial(_mixer_sample_kernel, pool_w=pool_w, lru_w=lru_w, conf_w=conf_w,
                             lru_taps=lru_taps, conf_taps=conf_taps, pool_buf=pool_buf,
                             start_pos=start_pos)
    return pl.pallas_call(
        kern,
        grid=(nt,),
        in_specs=[pl.BlockSpec(memory_space=pl.ANY),
                  pl.BlockSpec((None, rs, zc), rowmap)]
                 + [pl.BlockSpec((rs, s.shape[1]), bmap) for s in states]
                 + [pl.BlockSpec(a.shape, const3 if a.ndim == 3 else const2) for a in weights],
        out_specs=[pl.BlockSpec((None, rs, fw), rowmap)]
                  + [pl.BlockSpec((rs, s.shape[1]), bmap) for s in states],
        out_shape=[jax.ShapeDtypeStruct(f.shape, f.dtype)]
                  + [jax.ShapeDtypeStruct(s.shape, F32) for s in states],
        input_output_aliases={0: 0},
        scratch_shapes=[pltpu.VMEM((rs, conf_w), F32)],
        compiler_params=_params(vmem, 1),
        name="mixer_sample",
    )(f, z, *states, *weights)


def _gate_merge_kernel(xn_ref, f_ref, wga_ref, wgb_ref, wgc_ref, bga_ref, bgb_ref, bgc_ref,
                       wp_ref, wl_ref, wc_ref, o_ref, wg_bf, wb_bf, *, pool_w, lru_w):
    c1 = pool_w
    c2 = pool_w + lru_w

    @pl.when(pl.program_id(1) == 0)
    def _():
        wg_bf[0] = wga_ref[...].astype(BF16)
        wg_bf[1] = wgb_ref[...].astype(BF16)
        wg_bf[2] = wgc_ref[...].astype(BF16)
        wb_bf[0:c1, :] = wp_ref[...].astype(BF16)
        wb_bf[c1:c2, :] = wl_ref[...].astype(BF16)
        wb_bf[c2:, :] = wc_ref[...].astype(BF16)

    xn = xn_ref[...]
    ga = _sigmoid(jnp.dot(xn, wg_bf[0], preferred_element_type=F32) + bga_ref[...])
    m = ga * jnp.dot(f_ref[:, 0:c1], wb_bf[0:c1, :], preferred_element_type=F32)
    gb = _sigmoid(jnp.dot(xn, wg_bf[1], preferred_element_type=F32) + bgb_ref[...])
    m = m + gb * jnp.dot(f_ref[:, c1:c2], wb_bf[c1:c2, :], preferred_element_type=F32)
    gc = _sigmoid(jnp.dot(xn, wg_bf[2], preferred_element_type=F32) + bgc_ref[...])
    m = m + gc * jnp.dot(f_ref[:, c2:], wb_bf[c2:, :], preferred_element_type=F32)
    o_ref[...] = m.astype(BF16)


def _gate_merge(xn, f, wgate, bgate, wp, wl, wc, *, tn):
    nt, tm, d = xn.shape
    fw = f.shape[2]
    pool_w, lru_w = wp.shape[0], wl.shape[0]
    nd = d // tn
    vmem = (2 * tm * d * 2 + 2 * tm * fw * 2 + 2 * (3 * d + fw) * tn * 4 + (3 * d + fw) * tn * 2
            + 2 * tm * tn * 2 + 6 * tm * tn * 4)
    return pl.pallas_call(
        functools.partial(_gate_merge_kernel, pool_w=pool_w, lru_w=lru_w),
        grid=(nd, nt),
        in_specs=[
            pl.BlockSpec((None, tm, d), lambda j, i: (i, 0, 0)),
            pl.BlockSpec((None, tm, fw), lambda j, i: (i, 0, 0)),
            pl.BlockSpec((d, tn), lambda j, i: (0, j)),
            pl.BlockSpec((d, tn), lambda j, i: (0, nd + j)),
            pl.BlockSpec((d, tn), lambda j, i: (0, 2 * nd + j)),
            pl.BlockSpec((1, tn), lambda j, i: (0, j)),
            pl.BlockSpec((1, tn), lambda j, i: (0, nd + j)),
            pl.BlockSpec((1, tn), lambda j, i: (0, 2 * nd + j)),
            pl.BlockSpec((pool_w, tn), lambda j, i: (0, j)),
            pl.BlockSpec((lru_w, tn), lambda j, i: (0, j)),
            pl.BlockSpec((fw - pool_w - lru_w, tn), lambda j, i: (0, j)),
        ],
        out_specs=pl.BlockSpec((None, tm, tn), lambda j, i: (i, 0, j)),
        out_shape=jax.ShapeDtypeStruct((nt, tm, d), BF16),
        scratch_shapes=[pltpu.VMEM((3, d, tn), BF16), pltpu.VMEM((fw, tn), BF16)],
        compiler_params=_params(vmem, 2),
        name="gate_merge",
    )(xn, f, wgate, wgate, wgate, bgate, bgate, bgate, wp, wl, wc)


def _outproj_kernel(m_ref, w_ref, x_ref, o_ref, w_bf):
    @pl.when(pl.program_id(1) == 0)
    def _():
        w_bf[...] = w_ref[...].astype(BF16)

    o_ref[...] = x_ref[...] + jnp.dot(m_ref[...], w_bf[...], preferred_element_type=F32)


def _outproj(m, w, x, *, tn):
    nt, tm, d = x.shape
    k = m.shape[2]
    vmem = 2 * tm * k * 2 + 2 * k * tn * 4 + k * tn * 2 + 4 * tm * tn * 4 + tm * tn * 4
    return pl.pallas_call(
        _outproj_kernel,
        grid=(d // tn, nt),
        in_specs=[
            pl.BlockSpec((None, tm, k), lambda j, i: (i, 0, 0)),
            pl.BlockSpec((k, tn), lambda j, i: (0, j)),
            pl.BlockSpec((None, tm, tn), lambda j, i: (i, 0, j)),
        ],
        out_specs=pl.BlockSpec((None, tm, tn), lambda j, i: (i, 0, j)),
        out_shape=jax.ShapeDtypeStruct((nt, tm, d), F32),
        scratch_shapes=[pltpu.VMEM((k, tn), BF16)],
        compiler_params=_params(vmem, 2),
        name="outproj",
    )(m, w, x)


def _mlp_kernel(x_ref, g_ref, wup_ref, wdn_ref, o_ref, xn_ref):
    @pl.when(pl.program_id(1) == 0)
    def _():
        x = x_ref[...]
        xn_ref[...] = _rms_norm(x, g_ref[...]).astype(BF16)
        o_ref[...] = x

    hid = jnp.dot(xn_ref[...], wup_ref[...].astype(BF16), preferred_element_type=F32)
    hid = jnp.maximum(hid, 0.0)
    hid = (hid * hid).astype(BF16)
    o_ref[...] += jnp.dot(hid, wdn_ref[...].astype(BF16), preferred_element_type=F32)


def _mlp(x, g, wup, wdn, *, tf):
    nt, tm, d = x.shape
    dff = wup.shape[1]
    vmem = (tm * d * 4 + 2 * tm * d * 4 + tm * d * 2 + 4 * d * tf * 4 + 2 * d * tf * 2
            + 2 * tm * tf * 4)
    return pl.pallas_call(
        _mlp_kernel,
        grid=(nt, dff // tf),
        in_specs=[
            pl.BlockSpec((None, tm, d), lambda i, j: (i, 0, 0), pipeline_mode=pl.Buffered(1)),
            pl.BlockSpec((1, d), lambda i, j: (0, 0)),
            pl.BlockSpec((d, tf), lambda i, j: (0, j)),
            pl.BlockSpec((tf, d), lambda i, j: (j, 0)),
        ],
        out_specs=pl.BlockSpec((None, tm, d), lambda i, j: (i, 0, 0)),
        out_shape=jax.ShapeDtypeStruct((nt, tm, d), F32),
        scratch_shapes=[pltpu.VMEM((tm, d), BF16)],
        compiler_params=_params(vmem, 2),
        name="mlp",
    )(x, g, wup, wdn)


def _final_norm_kernel(x_ref, g_ref, yp_ref, ys_ref):
    rp = yp_ref.shape[0]
    y = _rms_norm(x_ref[...], g_ref[...])
    yp_ref[...] = y[0:rp, :]
    ys_ref[...] = y[rp:, :]


def _final_norm(x, g, *, rows_p):
    nt, tm, d = x.shape
    rs = tm - rows_p
    return pl.pallas_call(
        _final_norm_kernel,
        grid=(nt,),
        in_specs=[pl.BlockSpec((None, tm, d), lambda i: (i, 0, 0)),
                  pl.BlockSpec((1, d), lambda i: (0, 0))],
        out_specs=[pl.BlockSpec((rows_p, d), lambda i: (i, 0)),
                   pl.BlockSpec((rs, d), lambda i: (i, 0))],
        out_shape=[jax.ShapeDtypeStruct((nt * rows_p, d), F32),
                   jax.ShapeDtypeStruct((nt * rs, d), F32)],
        compiler_params=_params(5 * tm * d * 4, 1),
        name="final_norm",
    )(x, g)


def kernel(x_prompt, x_sample, state_pool, state_lru_conv, state_lru_h, state_conf_conv, g_mix, w_in, w_pool_grp, pool_scale, w_pool_br, w_lru_conv, b_lru_conv, w_lru_a, b_lru_a, w_lru_x, b_lru_x, lru_lambda, w_lru_br, w_conf_conv, b_conf_conv, g_conf, b_conf, w_conf_br, w_gate, b_gate, w_out, g_mlp, w_up, w_down, g_final):
    batch, seq, d = x_prompt.shape
    nb, dec_seq, _ = x_sample.shape
    assert dec_seq == 1
    depth = w_in.shape[0]
    pool_w = w_pool_br.shape[1]
    lru_w = w_lru_br.shape[1]
    conf_w = w_conf_br.shape[1]
    pool_buf = state_pool.shape[2]
    lru_buf = state_lru_conv.shape[2]
    conf_buf = state_conf_conv.shape[2]
    n_prompt = batch * seq
    rows_p = n_prompt // NT
    widths = dict(pool_w=pool_w, lru_w=lru_w, conf_w=conf_w)

    row = lambda a: a[:, None, :]
    g_mix_r, g_mlp_r, b_gate_r = row(g_mix), row(g_mlp), row(b_gate)
    pscale_r, blc_r, ba_r, bx_r, lam_r = row(pool_scale), row(b_lru_conv), row(b_lru_a), row(b_lru_x), row(lru_lambda)
    bcc_r, gcf_r, bcf_r = row(b_conf_conv), row(g_conf), row(b_conf)
    spool2 = state_pool.reshape(depth, nb, pool_buf * pool_w)
    slru2 = state_lru_conv.reshape(depth, nb, lru_buf * lru_w)
    sconf2 = state_conf_conv.reshape(depth, nb, conf_buf * conf_w)

    x = _entry(x_prompt.reshape(n_prompt, d), x_sample.reshape(nb, d))
    outs = [[] for _ in range(8)]
    for l in range(depth):
        z, xn = _inproj(x, g_mix_r[l], w_in[l], tn=512)
        mix_w = (w_pool_grp[l], pscale_r[l], w_lru_conv[l], blc_r[l], w_lru_a[l], w_lru_x[l],
                 ba_r[l], bx_r[l], lam_r[l], w_conf_conv[l], bcc_r[l], gcf_r[l], bcf_r[l])
        f, p_last, l_last, h_last, v_last = _mixer_prompt(
            z, *mix_w, batch=batch, seq=seq, rows_p=rows_p, tt=256, **widths)
        f, npool, nlru, nh, nconf = _mixer_sample(
            f, z, spool2[l], slru2[l], state_lru_h[l], sconf2[l], *mix_w,
            rows_p=rows_p, start_pos=PAST_LEN, **widths)
        merged = _gate_merge(xn, f, w_gate[l], b_gate_r[l], w_pool_br[l], w_lru_br[l],
                             w_conf_br[l], tn=256)
        x = _outproj(merged, w_out[l], x, tn=1024)
        x = _mlp(x, g_mlp_r[l], w_up[l], w_down[l], tf=512)

        outs[0].append(p_last[:, POOL_HALO - pool_buf:])
        outs[1].append(l_last[:, LRU_HALO - lru_buf:])
        outs[2].append(h_last[:, SUBLANE - 1])
        outs[3].append(v_last[:, CONF_HALO - conf_buf:])
        outs[4].append(npool.reshape(nb, pool_buf, pool_w))
        outs[5].append(nlru.reshape(nb, lru_buf, lru_w))
        outs[6].append(nh)
        outs[7].append(nconf.reshape(nb, conf_buf, conf_w))

    y_p, y_s = _final_norm(x, g_final[None, :], rows_p=rows_p)
    return (y_p.reshape(batch, seq, d), y_s.reshape(nb, 1, d)) + tuple(jnp.stack(o) for o in outs)
```

```python
import functools
import math

import jax
import jax.numpy as jnp
from jax import lax
from jax.experimental import pallas as pl
from jax.experimental.pallas import tpu as pltpu

F32 = jnp.float32
BF16 = jnp.bfloat16

POOL_WINDOWS = (2, 4, 8, 16)
LRU_C = 8.0
EPS = 1e-6
PAST_LEN = 16384
LANE = 128
SUBLANE = 8
NT = 8
VMEM_SLACK = 8 << 20
VMEM_CAP = 60 << 20


def _sigmoid(x):
    return 0.5 * jnp.tanh(0.5 * x) + 0.5


def _gelu_tanh(x):
    c = math.sqrt(2.0 / math.pi)
    return x * (0.5 * (1.0 + jnp.tanh(c * (x + 0.044715 * (x * x * x)))))


def _softplus(x):
    return jnp.maximum(x, 0.0) + jnp.log1p(jnp.exp(-jnp.abs(x)))


def _rms_norm(x, g):
    ms = jnp.mean(x * x, axis=-1, keepdims=True)
    return (x * lax.rsqrt(ms + EPS)) * g


def _layer_spec(a, l):
    zeros = (0,) * (a.ndim - 1)
    return pl.BlockSpec((None,) + a.shape[1:], lambda *_: (l,) + zeros)


def _params(vmem_bytes, n_axes):
    return pltpu.CompilerParams(
        dimension_semantics=("arbitrary",) * n_axes,
        vmem_limit_bytes=int(min(vmem_bytes + VMEM_SLACK, VMEM_CAP)))


def _entry_kernel(xp_ref, xs_ref, o_ref):
    rp = xp_ref.shape[0]
    o_ref[0:rp, :] = xp_ref[...]
    o_ref[rp:, :] = xs_ref[...]


def _entry(xp, xs):
    d = xp.shape[1]
    rp, rs = xp.shape[0] // NT, xs.shape[0] // NT
    return pl.pallas_call(
        _entry_kernel,
        grid=(NT,),
        in_specs=[pl.BlockSpec((rp, d), lambda i: (i, 0)),
                  pl.BlockSpec((rs, d), lambda i: (i, 0))],
        out_specs=pl.BlockSpec((None, rp + rs, d), lambda i: (i, 0, 0)),
        out_shape=jax.ShapeDtypeStruct((NT, rp + rs, d), F32),
        compiler_params=_params(4 * (rp + rs) * d * 4, 1),
        name="entry",
    )(xp, xs)


def _inproj_kernel(x_ref, g_ref, w_ref, z_ref, xn_ref):
    @pl.when(pl.program_id(1) == 0)
    def _():
        xn_ref[...] = _rms_norm(x_ref[...], g_ref[...]).astype(BF16)

    z_ref[...] = jnp.dot(xn_ref[...], w_ref[...].astype(BF16), preferred_element_type=F32)


def _inproj(x, g, w, l, *, tn):
    nt, tm, d = x.shape
    ncols = w.shape[2]
    vmem = tm * d * 4 + 2 * tm * d * 2 + 2 * d * tn * 4 + d * tn * 2 + 2 * tm * tn * 4
    return pl.pallas_call(
        _inproj_kernel,
        grid=(nt, ncols // tn),
        in_specs=[
            pl.BlockSpec((None, tm, d), lambda i, j: (i, 0, 0), pipeline_mode=pl.Buffered(1)),
            _layer_spec(g, l),
            pl.BlockSpec((None, d, tn), lambda i, j: (l, 0, j)),
        ],
        out_specs=[
            pl.BlockSpec((None, tm, tn), lambda i, j: (i, 0, j)),
            pl.BlockSpec((None, tm, d), lambda i, j: (i, 0, 0)),
        ],
        out_shape=[
            jax.ShapeDtypeStruct((nt, tm, ncols), F32),
            jax.ShapeDtypeStruct((nt, tm, d), BF16),
        ],
        compiler_params=_params(vmem, 2),
        name="inproj",
    )(x, g, w)


def _lru_coeffs(ri, ba, bx, sp):
    gate_r = _sigmoid(ri[:, :LANE] + ba)
    gate_i = _sigmoid(ri[:, LANE:] + bx)
    log_a = (-LRU_C) * gate_r * sp
    a = jnp.exp(log_a)
    th = jnp.tanh(log_a)
    mult = jnp.sqrt((-2.0 * th) / (1.0 - th))
    return a, gate_i, mult


def _layer_norm_silu(vc, g, b):
    mu = jnp.mean(vc, axis=-1, keepdims=True)
    cen = vc - mu
    var = jnp.mean(cen * cen, axis=-1, keepdims=True)
    y = (cen * lax.rsqrt(var + EPS)) * g + b
    return y * _sigmoid(y)


def _scan_tile(a, b, carry, tt):
    groups = tt // SUBLANE
    a3 = a.reshape(groups, SUBLANE, LANE)
    b3 = b.reshape(groups, SUBLANE, LANE)
    sub = lax.broadcasted_iota(jnp.int32, (groups, SUBLANE, LANE), 1)
    for k in (1, 2, 4):
        keep = sub >= k
        b_sh = jnp.where(keep, pltpu.roll(b3, k, 1), 0.0)
        a_sh = jnp.where(keep, pltpu.roll(a3, k, 1), 1.0)
        b3 = b3 + a3 * b_sh
        a3 = a3 * a_sh
    hs = []
    for j in range(groups):
        hj = b3[j] + a3[j] * carry
        hs.append(hj)
        carry = hj[SUBLANE - 1:SUBLANE, :]
    return jnp.concatenate(hs, axis=0)


POOL_HALO = 16
LRU_HALO = 8
CONF_HALO = 32
LN_ROWS = 64


def _mixer_prompt_kernel(z_ref, wgrp_ref, pscale_ref, wlc_ref, blc_ref, wa_ref, wx_ref, ba_ref,
                         bx_ref, lam_ref, wcc_ref, bcc_ref, gcf_ref, bcf_ref,
                         f_ref, plast_ref, llast_ref, hlast_ref, vlast_ref,
                         pool_ext, lru_ext, conf_ext, vc_buf, h_carry,
                         *, tt, pool_w, lru_w, conf_w, lru_taps, conf_taps):
    t = pl.program_id(1)
    c_lru = pool_w
    c_gel = c_lru + lru_w
    c_ca = c_gel + lru_w
    c_cb = c_ca + conf_w

    @pl.when(t == 0)
    def _():
        pool_ext[0:POOL_HALO, :] = jnp.zeros((POOL_HALO, pool_w), F32)
        lru_ext[0:LRU_HALO, :] = jnp.zeros((LRU_HALO, lru_w), F32)
        conf_ext[0:CONF_HALO, :] = jnp.zeros((CONF_HALO, conf_w), F32)
        h_carry[...] = jnp.zeros_like(h_carry)

    row = lax.broadcasted_iota(jnp.int32, (tt, LANE), 0)
    pos = row + t * tt
    posf = pos.astype(F32)
    is_first = pos == 0

    pool_ext[POOL_HALO:POOL_HALO + tt, :] = z_ref[:, 0:pool_w]
    for g, w in enumerate(POOL_WINDOWS):
        lanes = slice(g * LANE, (g + 1) * LANE)
        u = pool_ext[POOL_HALO:POOL_HALO + tt, lanes]
        s = u
        for j in range(1, w):
            s = s + pool_ext[POOL_HALO - j:POOL_HALO - j + tt, lanes]
        cnt = jnp.minimum(jnp.float32(w), posf + 1.0)
        d = s / cnt - u
        y = jnp.dot(d.astype(BF16), wgrp_ref[g].astype(BF16), preferred_element_type=F32)
        f_ref[:, lanes] = (y * pscale_ref[:, lanes]).astype(BF16)
    plast_ref[...] = pool_ext[tt:tt + POOL_HALO, :]
    pool_ext[0:POOL_HALO, :] = pool_ext[tt:tt + POOL_HALO, :]

    lru_ext[LRU_HALO:LRU_HALO + tt, :] = z_ref[:, c_lru:c_lru + lru_w]
    sp_all = _softplus(-lam_ref[...])
    for n in range(lru_w // LANE):
        lanes = slice(n * LANE, (n + 1) * LANE)
        xc = blc_ref[:, lanes] + wlc_ref[lru_taps - 1:lru_taps, lanes] * lru_ext[LRU_HALO:LRU_HALO + tt, lanes]
        for k in range(lru_taps - 1):
            off = LRU_HALO - (lru_taps - 1) + k
            xc = xc + wlc_ref[k:k + 1, lanes] * lru_ext[off:off + tt, lanes]
        xcb = xc.astype(BF16)
        ri = jnp.concatenate(
            [jnp.dot(xcb, wa_ref[n].astype(BF16), preferred_element_type=F32),
             jnp.dot(xcb, wx_ref[n].astype(BF16), preferred_element_type=F32)], axis=1)
        a, gate_i, mult = _lru_coeffs(ri, ba_ref[:, lanes], bx_ref[:, lanes], sp_all[:, lanes])
        a = jnp.where(is_first, 0.0, a)
        mult = jnp.where(is_first, 1.0, mult)
        b = xc * gate_i * mult
        h = _scan_tile(a, b, h_carry[SUBLANE - 1:SUBLANE, lanes], tt)
        h_carry[:, lanes] = h[tt - SUBLANE:tt, :]
        hlast_ref[:, lanes] = h[tt - SUBLANE:tt, :]
        gel = _gelu_tanh(z_ref[:, c_gel + n * LANE:c_gel + (n + 1) * LANE])
        f_ref[:, pool_w + n * LANE:pool_w + (n + 1) * LANE] = (h * gel).astype(BF16)
    llast_ref[...] = lru_ext[tt:tt + LRU_HALO, :]
    lru_ext[0:LRU_HALO, :] = lru_ext[tt:tt + LRU_HALO, :]

    conf_ext[CONF_HALO:CONF_HALO + tt, :] = (
        z_ref[:, c_ca:c_ca + conf_w] * _sigmoid(z_ref[:, c_cb:c_cb + conf_w]))
    base = CONF_HALO - (conf_taps - 1)
    for c in range(conf_w // LANE):
        lanes = slice(c * LANE, (c + 1) * LANE)
        acc = None
        for r in range(SUBLANE):
            offs = [o for o in range(base, CONF_HALO + 1) if o % SUBLANE == r]
            if not offs:
                continue
            part = None
            for o in offs:
                k = o - base
                term = wcc_ref[k:k + 1, lanes] * conf_ext[o:o + tt, lanes]
                part = term if part is None else part + term
            acc = part if acc is None else acc + part
        vc_buf[:, lanes] = acc + bcc_ref[:, lanes]
    for r in range(tt // LN_ROWS):
        rows = slice(r * LN_ROWS, (r + 1) * LN_ROWS)
        y = _layer_norm_silu(vc_buf[rows, :], gcf_ref[...], bcf_ref[...])
        f_ref[rows, pool_w + lru_w:pool_w + lru_w + conf_w] = y.astype(BF16)
    vlast_ref[...] = conf_ext[tt:tt + CONF_HALO, :]
    conf_ext[0:CONF_HALO, :] = conf_ext[tt:tt + CONF_HALO, :]


def _mixer_prompt(z, wgrp, pscale, wlc, blc, wa, wx, ba, bx, lam, wcc, bcc, gcf, bcf, l,
                  *, batch, seq, rows_p, tt, pool_w, lru_w, conf_w):
    nt, tm, zc = z.shape
    tiles_per_seq = seq // rows_p
    steps_per_tile = rows_p // tt
    fw = pool_w + lru_w + conf_w
    lru_taps = wlc.shape[1]
    conf_taps = wcc.shape[1]
    zmap = lambda b, t: (b * tiles_per_seq + t // steps_per_tile, t % steps_per_tile, 0)
    smap = lambda b, t: (b, 0, 0)
    vmem = (2 * tt * zc * 4 + 2 * tt * fw * 2
            + (POOL_HALO + tt) * pool_w * 4 + (LRU_HALO + tt) * lru_w * 4
            + (CONF_HALO + tt) * conf_w * 4 + tt * conf_w * 4 + (8 << 20))
    kern = functools.partial(_mixer_prompt_kernel, tt=tt, pool_w=pool_w, lru_w=lru_w,
                             conf_w=conf_w, lru_taps=lru_taps, conf_taps=conf_taps)
    weights = (wgrp, pscale, wlc, blc, wa, wx, ba, bx, lam, wcc, bcc, gcf, bcf)
    return pl.pallas_call(
        kern,
        grid=(batch, seq // tt),
        in_specs=[pl.BlockSpec((None, tt, zc), zmap)] + [_layer_spec(a, l) for a in weights],
        out_specs=[
            pl.BlockSpec((None, tt, fw), zmap),
            pl.BlockSpec((None, POOL_HALO, pool_w), smap),
            pl.BlockSpec((None, LRU_HALO, lru_w), smap),
            pl.BlockSpec((None, SUBLANE, lru_w), smap),
            pl.BlockSpec((None, CONF_HALO, conf_w), smap),
        ],
        out_shape=[
            jax.ShapeDtypeStruct((nt, tm, fw), BF16),
            jax.ShapeDtypeStruct((batch, POOL_HALO, pool_w), F32),
            jax.ShapeDtypeStruct((batch, LRU_HALO, lru_w), F32),
            jax.ShapeDtypeStruct((batch, SUBLANE, lru_w), F32),
            jax.ShapeDtypeStruct((batch, CONF_HALO, conf_w), F32),
        ],
        scratch_shapes=[
            pltpu.VMEM((POOL_HALO + tt, pool_w), F32),
            pltpu.VMEM((LRU_HALO + tt, lru_w), F32),
            pltpu.VMEM((CONF_HALO + tt, conf_w), F32),
            pltpu.VMEM((tt, conf_w), F32),
            pltpu.VMEM((SUBLANE, lru_w), F32),
        ],
        compiler_params=_params(vmem, 2),
        name="mixer_prompt",
    )(z, *weights)


def _mixer_sample_kernel(f_any, z_ref, spool_ref, slru_ref, sh_ref, sconf_ref,
                         wgrp_ref, pscale_ref, wlc_ref, blc_ref, wa_ref, wx_ref, ba_ref,
                         bx_ref, lam_ref, wcc_ref, bcc_ref, gcf_ref, bcf_ref,
                         f_ref, npool_ref, nlru_ref, hnew_ref, nconf_ref, vc_buf,
                         *, pool_w, lru_w, conf_w, lru_taps, conf_taps, pool_buf, start_pos):
    del f_any
    c_lru = pool_w
    c_gel = c_lru + lru_w
    c_ca = c_gel + lru_w
    c_cb = c_ca + conf_w

    for g, w in enumerate(POOL_WINDOWS):
        lanes = slice(g * LANE, (g + 1) * LANE)
        u = z_ref[:, lanes]
        s = u
        for j in range(1, w):
            trow = pool_buf - j
            s = s + spool_ref[:, trow * pool_w + g * LANE:trow * pool_w + (g + 1) * LANE]
        cnt = float(min(w, start_pos + 1))
        d = s / cnt - u
        y = jnp.dot(d.astype(BF16), wgrp_ref[g].astype(BF16), preferred_element_type=F32)
        f_ref[:, lanes] = (y * pscale_ref[:, lanes]).astype(BF16)
    npool_ref[:, 0:(pool_buf - 1) * pool_w] = spool_ref[:, pool_w:]
    npool_ref[:, (pool_buf - 1) * pool_w:] = z_ref[:, 0:pool_w]

    sp_all = _softplus(-lam_ref[...])
    for n in range(lru_w // LANE):
        lanes = slice(n * LANE, (n + 1) * LANE)
        xc = blc_ref[:, lanes] + wlc_ref[lru_taps - 1:lru_taps, lanes] * z_ref[:, c_lru + n * LANE:c_lru + (n + 1) * LANE]
        for k in range(lru_taps - 1):
            xc = xc + wlc_ref[k:k + 1, lanes] * slru_ref[:, k * lru_w + n * LANE:k * lru_w + (n + 1) * LANE]
        xcb = xc.astype(BF16)
        ri = jnp.concatenate(
            [jnp.dot(xcb, wa_ref[n].astype(BF16), preferred_element_type=F32),
             jnp.dot(xcb, wx_ref[n].astype(BF16), preferred_element_type=F32)], axis=1)
        a, gate_i, mult = _lru_coeffs(ri, ba_ref[:, lanes], bx_ref[:, lanes], sp_all[:, lanes])
        if start_pos == 0:
            a = jnp.zeros_like(a)
            mult = jnp.ones_like(mult)
        h = a * sh_ref[:, lanes] + xc * gate_i * mult
        hnew_ref[:, lanes] = h
        gel = _gelu_tanh(z_ref[:, c_gel + n * LANE:c_gel + (n + 1) * LANE])
        f_ref[:, pool_w + n * LANE:pool_w + (n + 1) * LANE] = (h * gel).astype(BF16)
    nlru_ref[:, 0:(lru_taps - 2) * lru_w] = slru_ref[:, lru_w:]
    nlru_ref[:, (lru_taps - 2) * lru_w:] = z_ref[:, c_lru:c_lru + lru_w]

    for c in range(conf_w // LANE):
        lanes = slice(c * LANE, (c + 1) * LANE)
        v = z_ref[:, c_ca + c * LANE:c_ca + (c + 1) * LANE] * _sigmoid(
            z_ref[:, c_cb + c * LANE:c_cb + (c + 1) * LANE])
        nconf_ref[:, (conf_taps - 2) * conf_w + c * LANE:(conf_taps - 2) * conf_w + (c + 1) * LANE] = v
        acc = bcc_ref[:, lanes] + wcc_ref[conf_taps - 1:conf_taps, lanes] * v
        for k in range(conf_taps - 1):
            acc = acc + wcc_ref[k:k + 1, lanes] * sconf_ref[:, k * conf_w + c * LANE:k * conf_w + (c + 1) * LANE]
        vc_buf[:, lanes] = acc
    nconf_ref[:, 0:(conf_taps - 2) * conf_w] = sconf_ref[:, conf_w:]
    y = _layer_norm_silu(vc_buf[...], gcf_ref[...], bcf_ref[...])
    f_ref[:, pool_w + lru_w:pool_w + lru_w + conf_w] = y.astype(BF16)


def _mixer_sample(f, z, spool, slru, sh, sconf, wgrp, pscale, wlc, blc, wa, wx, ba, bx, lam,
                  wcc, bcc, gcf, bcf, l, *, rows_p, pool_w, lru_w, conf_w, start_pos):
    nt, tm, zc = z.shape
    rs = tm - rows_p
    fw = pool_w + lru_w + conf_w
    lru_taps = wlc.shape[1]
    conf_taps = wcc.shape[1]
    pool_buf = spool.shape[2] // pool_w
    rowmap = lambda i: (i, rows_p // rs, 0)
    bmap = lambda i: (i, 0)
    states = (spool, slru, sh, sconf)
    weights = (wgrp, pscale, wlc, blc, wa, wx, ba, bx, lam, wcc, bcc, gcf, bcf)
    vmem = (4 * rs * sum(s.shape[2] for s in states) * 4 * 2
            + 2 * sum(a[0].size * a.dtype.itemsize for a in weights) + (8 << 20))
    kern = functools.partial(_mixer_sample_kernel, pool_w=pool_w, lru_w=lru_w, conf_w=conf_w,
                             lru_taps=lru_taps, conf_taps=conf_taps, pool_buf=pool_buf,
                             start_pos=start_pos)
    return pl.pallas_call(
        kern,
        grid=(nt,),
        in_specs=[pl.BlockSpec(memory_space=pl.ANY),
                  pl.BlockSpec((None, rs, zc), rowmap)]
                 + [pl.BlockSpec((None, rs, s.shape[2]), lambda i: (l, i, 0)) for s in states]
                 + [_layer_spec(a, l) for a in weights],
        out_specs=[pl.BlockSpec((None, rs, fw), rowmap)]
                  + [pl.BlockSpec((rs, s.shape[2]), bmap) for s in states],
        out_shape=[jax.ShapeDtypeStruct(f.shape, f.dtype)]
                  + [jax.ShapeDtypeStruct(s.shape[1:], F32) for s in states],
        input_output_aliases={0: 0},
        scratch_shapes=[pltpu.VMEM((rs, conf_w), F32)],
        compiler_params=_params(vmem, 1),
        name="mixer_sample",
    )(f, z, *states, *weights)


def _gate_merge_kernel(xn_ref, f_ref, wga_ref, wgb_ref, wgc_ref, bga_ref, bgb_ref, bgc_ref,
                       wp_ref, wl_ref, wc_ref, o_ref, wg_bf, wb_bf, *, pool_w, lru_w):
    c1 = pool_w
    c2 = pool_w + lru_w

    @pl.when(pl.program_id(1) == 0)
    def _():
        wg_bf[0] = wga_ref[...].astype(BF16)
        wg_bf[1] = wgb_ref[...].astype(BF16)
        wg_bf[2] = wgc_ref[...].astype(BF16)
        wb_bf[0:c1, :] = wp_ref[...].astype(BF16)
        wb_bf[c1:c2, :] = wl_ref[...].astype(BF16)
        wb_bf[c2:, :] = wc_ref[...].astype(BF16)

    xn = xn_ref[...]
    ga = _sigmoid(jnp.dot(xn, wg_bf[0], preferred_element_type=F32) + bga_ref[...])
    m = ga * jnp.dot(f_ref[:, 0:c1], wb_bf[0:c1, :], preferred_element_type=F32)
    gb = _sigmoid(jnp.dot(xn, wg_bf[1], preferred_element_type=F32) + bgb_ref[...])
    m = m + gb * jnp.dot(f_ref[:, c1:c2], wb_bf[c1:c2, :], preferred_element_type=F32)
    gc = _sigmoid(jnp.dot(xn, wg_bf[2], preferred_element_type=F32) + bgc_ref[...])
    m = m + gc * jnp.dot(f_ref[:, c2:], wb_bf[c2:, :], preferred_element_type=F32)
    o_ref[...] = m.astype(BF16)


def _gate_merge(xn, f, wgate, bgate, wp, wl, wc, l, *, tn):
    nt, tm, d = xn.shape
    fw = f.shape[2]
    pool_w, lru_w = wp.shape[1], wl.shape[1]
    nd = d // tn
    vmem = (2 * tm * d * 2 + 2 * tm * fw * 2 + 2 * (3 * d + fw) * tn * 4 + (3 * d + fw) * tn * 2
            + 2 * tm * tn * 2 + 6 * tm * tn * 4)
    return pl.pallas_call(
        functools.partial(_gate_merge_kernel, pool_w=pool_w, lru_w=lru_w),
        grid=(nd, nt),
        in_specs=[
            pl.BlockSpec((None, tm, d), lambda j, i: (i, 0, 0)),
            pl.BlockSpec((None, tm, fw), lambda j, i: (i, 0, 0)),
            pl.BlockSpec((None, d, tn), lambda j, i: (l, 0, j)),
            pl.BlockSpec((None, d, tn), lambda j, i: (l, 0, nd + j)),
            pl.BlockSpec((None, d, tn), lambda j, i: (l, 0, 2 * nd + j)),
            pl.BlockSpec((None, 1, tn), lambda j, i: (l, 0, j)),
            pl.BlockSpec((None, 1, tn), lambda j, i: (l, 0, nd + j)),
            pl.BlockSpec((None, 1, tn), lambda j, i: (l, 0, 2 * nd + j)),
            pl.BlockSpec((None, pool_w, tn), lambda j, i: (l, 0, j)),
            pl.BlockSpec((None, lru_w, tn), lambda j, i: (l, 0, j)),
            pl.BlockSpec((None, fw - pool_w - lru_w, tn), lambda j, i: (l, 0, j)),
        ],
        out_specs=pl.BlockSpec((None, tm, tn), lambda j, i: (i, 0, j)),
        out_shape=jax.ShapeDtypeStruct((nt, tm, d), BF16),
        scratch_shapes=[pltpu.VMEM((3, d, tn), BF16), pltpu.VMEM((fw, tn), BF16)],
        compiler_params=_params(vmem, 2),
        name="gate_merge",
    )(xn, f, wgate, wgate, wgate, bgate, bgate, bgate, wp, wl, wc)


def _outproj_kernel(m_ref, w_ref, x_ref, o_ref, w_bf):
    @pl.when(pl.program_id(1) == 0)
    def _():
        w_bf[...] = w_ref[...].astype(BF16)

    o_ref[...] = x_ref[...] + jnp.dot(m_ref[...], w_bf[...], preferred_element_type=F32)


def _outproj(m, w, x, l, *, tn):
    nt, tm, d = x.shape
    k = m.shape[2]
    vmem = 2 * tm * k * 2 + 2 * k * tn * 4 + k * tn * 2 + 4 * tm * tn * 4 + tm * tn * 4
    return pl.pallas_call(
        _outproj_kernel,
        grid=(d // tn, nt),
        in_specs=[
            pl.BlockSpec((None, tm, k), lambda j, i: (i, 0, 0)),
            pl.BlockSpec((None, k, tn), lambda j, i: (l, 0, j)),
            pl.BlockSpec((None, tm, tn), lambda j, i: (i, 0, j)),
        ],
        out_specs=pl.BlockSpec((None, tm, tn), lambda j, i: (i, 0, j)),
        out_shape=jax.ShapeDtypeStruct((nt, tm, d), F32),
        scratch_shapes=[pltpu.VMEM((k, tn), BF16)],
        compiler_params=_params(vmem, 2),
        name="outproj",
    )(m, w, x)


def _mlp_kernel(x_ref, g_ref, wup_ref, wdn_ref, o_ref, xn_ref):
    @pl.when(pl.program_id(1) == 0)
    def _():
        x = x_ref[...]
        xn_ref[...] = _rms_norm(x, g_ref[...]).astype(BF16)
        o_ref[...] = x

    hid = jnp.dot(xn_ref[...], wup_ref[...].astype(BF16), preferred_element_type=F32)
    hid = jnp.maximum(hid, 0.0)
    hid = (hid * hid).astype(BF16)
    o_ref[...] += jnp.dot(hid, wdn_ref[...].astype(BF16), preferred_element_type=F32)


def _mlp(x, g, wup, wdn, l, *, tf):
    nt, tm, d = x.shape
    dff = wup.shape[2]
    vmem = (tm * d * 4 + 2 * tm * d * 4 + tm * d * 2 + 4 * d * tf * 4 + 2 * d * tf * 2
            + 2 * tm * tf * 4)
    return pl.pallas_call(
        _mlp_kernel,
        grid=(nt, dff // tf),
        in_specs=[
            pl.BlockSpec((None, tm, d), lambda i, j: (i, 0, 0), pipeline_mode=pl.Buffered(1)),
            _layer_spec(g, l),
            pl.BlockSpec((None, d, tf), lambda i, j: (l, 0, j)),
            pl.BlockSpec((None, tf, d), lambda i, j: (l, j, 0)),
        ],
        out_specs=pl.BlockSpec((None, tm, d), lambda i, j: (i, 0, 0)),
        out_shape=jax.ShapeDtypeStruct((nt, tm, d), F32),
        scratch_shapes=[pltpu.VMEM((tm, d), BF16)],
        compiler_params=_params(vmem, 2),
        name="mlp",
    )(x, g, wup, wdn)


def _final_norm_kernel(x_ref, g_ref, yp_ref, ys_ref):
    rp = yp_ref.shape[0]
    y = _rms_norm(x_ref[...], g_ref[...])
    yp_ref[...] = y[0:rp, :]
    ys_ref[...] = y[rp:, :]


def _final_norm(x, g, *, rows_p):
    nt, tm, d = x.shape
    rs = tm - rows_p
    return pl.pallas_call(
        _final_norm_kernel,
        grid=(nt,),
        in_specs=[pl.BlockSpec((None, tm, d), lambda i: (i, 0, 0)),
                  pl.BlockSpec((1, d), lambda i: (0, 0))],
        out_specs=[pl.BlockSpec((rows_p, d), lambda i: (i, 0)),
                   pl.BlockSpec((rs, d), lambda i: (i, 0))],
        out_shape=[jax.ShapeDtypeStruct((nt * rows_p, d), F32),
                   jax.ShapeDtypeStruct((nt * rs, d), F32)],
        compiler_params=_params(5 * tm * d * 4, 1),
        name="final_norm",
    )(x, g)


def kernel(x_prompt, x_sample, state_pool, state_lru_conv, state_lru_h, state_conf_conv, g_mix, w_in, w_pool_grp, pool_scale, w_pool_br, w_lru_conv, b_lru_conv, w_lru_a, b_lru_a, w_lru_x, b_lru_x, lru_lambda, w_lru_br, w_conf_conv, b_conf_conv, g_conf, b_conf, w_conf_br, w_gate, b_gate, w_out, g_mlp, w_up, w_down, g_final):
    batch, seq, d = x_prompt.shape
    nb, dec_seq, _ = x_sample.shape
    assert dec_seq == 1
    depth = w_in.shape[0]
    pool_w = w_pool_br.shape[1]
    lru_w = w_lru_br.shape[1]
    conf_w = w_conf_br.shape[1]
    pool_buf = state_pool.shape[2]
    lru_buf = state_lru_conv.shape[2]
    conf_buf = state_conf_conv.shape[2]
    n_prompt = batch * seq
    rows_p = n_prompt // NT
    widths = dict(pool_w=pool_w, lru_w=lru_w, conf_w=conf_w)

    row = lambda a: a[:, None, :]
    g_mix_r, g_mlp_r, b_gate_r = row(g_mix), row(g_mlp), row(b_gate)
    pscale_r, blc_r, ba_r, bx_r, lam_r = row(pool_scale), row(b_lru_conv), row(b_lru_a), row(b_lru_x), row(lru_lambda)
    bcc_r, gcf_r, bcf_r = row(b_conf_conv), row(g_conf), row(b_conf)
    spool2 = state_pool.reshape(depth, nb, pool_buf * pool_w)
    slru2 = state_lru_conv.reshape(depth, nb, lru_buf * lru_w)
    sconf2 = state_conf_conv.reshape(depth, nb, conf_buf * conf_w)

    x = _entry(x_prompt.reshape(n_prompt, d), x_sample.reshape(nb, d))
    outs = [[] for _ in range(8)]
    mix_w = (w_pool_grp, pscale_r, w_lru_conv, blc_r, w_lru_a, w_lru_x,
             ba_r, bx_r, lam_r, w_conf_conv, bcc_r, gcf_r, bcf_r)
    for l in range(depth):
        z, xn = _inproj(x, g_mix_r, w_in, l, tn=512)
        f, p_last, l_last, h_last, v_last = _mixer_prompt(
            z, *mix_w, l, batch=batch, seq=seq, rows_p=rows_p, tt=256, **widths)
        f, npool, nlru, nh, nconf = _mixer_sample(
            f, z, spool2, slru2, state_lru_h, sconf2, *mix_w, l,
            rows_p=rows_p, start_pos=PAST_LEN, **widths)
        merged = _gate_merge(xn, f, w_gate, b_gate_r, w_pool_br, w_lru_br, w_conf_br, l, tn=256)
        x = _outproj(merged, w_out, x, l, tn=1024)
        x = _mlp(x, g_mlp_r, w_up, w_down, l, tf=512)

        outs[0].append(p_last[:, POOL_HALO - pool_buf:])
        outs[1].append(l_last[:, LRU_HALO - lru_buf:])
        outs[2].append(h_last[:, SUBLANE - 1])
        outs[3].append(v_last[:, CONF_HALO - conf_buf:])
        outs[4].append(npool.reshape(nb, pool_buf, pool_w))
        outs[5].append(nlru.reshape(nb, lru_buf, lru_w))
        outs[6].append(nh)
        outs[7].append(nconf.reshape(nb, conf_buf, conf_w))

    y_p, y_s = _final_norm(x, g_final[None, :], rows_p=rows_p)
    return (y_p.reshape(batch, seq, d), y_s.reshape(nb, 1, d)) + tuple(jnp.stack(o) for o in outs)
```

```python
import functools
import math

import jax
import jax.numpy as jnp
from jax import lax
from jax.experimental import pallas as pl
from jax.experimental.pallas import tpu as pltpu

F32 = jnp.float32
BF16 = jnp.bfloat16

POOL_WINDOWS = (2, 4, 8, 16)
LRU_C = 8.0
EPS = 1e-6
PAST_LEN = 16384
LANE = 128
SUBLANE = 8
NT = 8
VMEM_SLACK = 8 << 20
VMEM_CAP = 60 << 20


def _sigmoid(x):
    return 0.5 * jnp.tanh(0.5 * x) + 0.5


def _gelu_tanh(x):
    c = math.sqrt(2.0 / math.pi)
    return x * (0.5 * (1.0 + jnp.tanh(c * (x + 0.044715 * (x * x * x)))))


def _softplus(x):
    return jnp.maximum(x, 0.0) + jnp.log1p(jnp.exp(-jnp.abs(x)))


def _rms_norm(x, g):
    ms = jnp.mean(x * x, axis=-1, keepdims=True)
    return (x * lax.rsqrt(ms + EPS)) * g


def _layer_spec(a, l):
    zeros = (0,) * (a.ndim - 1)
    return pl.BlockSpec((None,) + a.shape[1:], lambda *_: (l,) + zeros)


def _params(vmem_bytes, n_axes):
    return pltpu.CompilerParams(
        dimension_semantics=("arbitrary",) * n_axes,
        vmem_limit_bytes=int(min(vmem_bytes + VMEM_SLACK, VMEM_CAP)))


def _entry_kernel(xp_ref, xs_ref, o_ref):
    rp = xp_ref.shape[0]
    o_ref[0:rp, :] = xp_ref[...]
    o_ref[rp:, :] = xs_ref[...]


def _entry(xp, xs):
    d = xp.shape[1]
    rp, rs = xp.shape[0] // NT, xs.shape[0] // NT
    return pl.pallas_call(
        _entry_kernel,
        grid=(NT,),
        in_specs=[pl.BlockSpec((rp, d), lambda i: (i, 0)),
                  pl.BlockSpec((rs, d), lambda i: (i, 0))],
        out_specs=pl.BlockSpec((None, rp + rs, d), lambda i: (i, 0, 0)),
        out_shape=jax.ShapeDtypeStruct((NT, rp + rs, d), F32),
        compiler_params=_params(4 * (rp + rs) * d * 4, 1),
        name="entry",
    )(xp, xs)


def _cast_kernel(w_ref, o_ref):
    o_ref[...] = w_ref[...].astype(BF16)


def _cast_bf16(w, *, rows):
    depth, k, n = w.shape
    return pl.pallas_call(
        _cast_kernel,
        grid=(depth, k // rows),
        in_specs=[pl.BlockSpec((None, rows, n), lambda l, i: (l, i, 0))],
        out_specs=pl.BlockSpec((None, rows, n), lambda l, i: (l, i, 0)),
        out_shape=jax.ShapeDtypeStruct(w.shape, BF16),
        compiler_params=_params(2 * rows * n * 6, 2),
        name="cast_w_in",
    )(w)


def _lru_coeffs(ri, ba, bx, sp):
    gate_r = _sigmoid(ri[:, :LANE] + ba)
    gate_i = _sigmoid(ri[:, LANE:] + bx)
    log_a = (-LRU_C) * gate_r * sp
    a = jnp.exp(log_a)
    th = jnp.tanh(log_a)
    mult = jnp.sqrt((-2.0 * th) / (1.0 - th))
    return a, gate_i, mult


def _layer_norm_silu(vc, g, b):
    mu = jnp.mean(vc, axis=-1, keepdims=True)
    cen = vc - mu
    var = jnp.mean(cen * cen, axis=-1, keepdims=True)
    y = (cen * lax.rsqrt(var + EPS)) * g + b
    return y * _sigmoid(y)


def _scan_tile(a, b, carry, tt):
    groups = tt // SUBLANE
    a3 = a.reshape(groups, SUBLANE, LANE)
    b3 = b.reshape(groups, SUBLANE, LANE)
    sub = lax.broadcasted_iota(jnp.int32, (groups, SUBLANE, LANE), 1)
    for k in (1, 2, 4):
        keep = sub >= k
        b_sh = jnp.where(keep, pltpu.roll(b3, k, 1), 0.0)
        a_sh = jnp.where(keep, pltpu.roll(a3, k, 1), 1.0)
        b3 = b3 + a3 * b_sh
        a3 = a3 * a_sh
    hs = []
    for j in range(groups):
        hj = b3[j] + a3[j] * carry
        hs.append(hj)
        carry = hj[SUBLANE - 1:SUBLANE, :]
    return jnp.concatenate(hs, axis=0)


POOL_HALO = 16
LRU_HALO = 8
CONF_HALO = 32
LN_ROWS = 64


def _mix_reset(pool_ext, lru_ext, conf_ext, h_carry):
    pool_ext[0:POOL_HALO, :] = jnp.zeros((POOL_HALO, pool_ext.shape[1]), F32)
    lru_ext[0:LRU_HALO, :] = jnp.zeros((LRU_HALO, lru_ext.shape[1]), F32)
    conf_ext[0:CONF_HALO, :] = jnp.zeros((CONF_HALO, conf_ext.shape[1]), F32)
    h_carry[...] = jnp.zeros_like(h_carry)


def _mix_tile(t, z_ref, wgrp_ref, pscale_ref, wlc_ref, blc_ref, wa_ref, wx_ref, ba_ref,
              bx_ref, lam_ref, wcc_ref, bcc_ref, gcf_ref, bcf_ref,
              f_ref, plast_ref, llast_ref, hlast_ref, vlast_ref,
              pool_ext, lru_ext, conf_ext, vc_buf, h_carry,
              *, tt, pool_w, lru_w, conf_w, lru_taps, conf_taps, tick):
    c_lru = pool_w
    c_gel = c_lru + lru_w
    c_ca = c_gel + lru_w
    c_cb = c_ca + conf_w

    row = lax.broadcasted_iota(jnp.int32, (tt, LANE), 0)
    pos = row + t * tt
    posf = pos.astype(F32)
    is_first = pos == 0
    never = posf[0:SUBLANE, :] < 0.0

    tick()
    pool_ext[POOL_HALO:POOL_HALO + tt, :] = z_ref[:, 0:pool_w]
    for g, w in enumerate(POOL_WINDOWS):
        lanes = slice(g * LANE, (g + 1) * LANE)
        u = pool_ext[POOL_HALO:POOL_HALO + tt, lanes]
        s = u
        for j in range(1, w):
            s = s + pool_ext[POOL_HALO - j:POOL_HALO - j + tt, lanes]
        cnt = jnp.minimum(jnp.float32(w), posf + 1.0)
        d = s / cnt - u
        y = jnp.dot(d.astype(BF16), wgrp_ref[g].astype(BF16), preferred_element_type=F32)
        f_ref[:, lanes] = (y * pscale_ref[:, lanes]).astype(BF16)
    plast_ref[...] = pool_ext[tt:tt + POOL_HALO, :]
    pool_ext[0:POOL_HALO, :] = pool_ext[tt:tt + POOL_HALO, :]

    lru_ext[LRU_HALO:LRU_HALO + tt, :] = z_ref[:, c_lru:c_lru + lru_w]
    sp_all = _softplus(-lam_ref[...])
    for n in range(lru_w // LANE):
        lanes = slice(n * LANE, (n + 1) * LANE)
        xc = blc_ref[:, lanes] + wlc_ref[lru_taps - 1:lru_taps, lanes] * lru_ext[LRU_HALO:LRU_HALO + tt, lanes]
        for k in range(lru_taps - 1):
            off = LRU_HALO - (lru_taps - 1) + k
            xc = xc + wlc_ref[k:k + 1, lanes] * lru_ext[off:off + tt, lanes]
        xcb = xc.astype(BF16)
        ri = jnp.concatenate(
            [jnp.dot(xcb, wa_ref[n].astype(BF16), preferred_element_type=F32),
             jnp.dot(xcb, wx_ref[n].astype(BF16), preferred_element_type=F32)], axis=1)
        tick()
        a, gate_i, mult = _lru_coeffs(ri, ba_ref[:, lanes], bx_ref[:, lanes], sp_all[:, lanes])
        a = jnp.where(is_first, 0.0, a)
        mult = jnp.where(is_first, 1.0, mult)
        b = xc * gate_i * mult
        h = _scan_tile(a, b, h_carry[SUBLANE - 1:SUBLANE, lanes], tt)
        h_carry[:, lanes] = h[tt - SUBLANE:tt, :]
        hlast_ref[:, lanes] = h[tt - SUBLANE:tt, :]
        gel = _gelu_tanh(z_ref[:, c_gel + n * LANE:c_gel + (n + 1) * LANE])
        f_ref[:, pool_w + n * LANE:pool_w + (n + 1) * LANE] = (h * gel).astype(BF16)
    llast_ref[...] = lru_ext[tt:tt + LRU_HALO, :]
    lru_ext[0:LRU_HALO, :] = lru_ext[tt:tt + LRU_HALO, :]

    conf_ext[CONF_HALO:CONF_HALO + tt, :] = (
        z_ref[:, c_ca:c_ca + conf_w] * _sigmoid(z_ref[:, c_cb:c_cb + conf_w]))
    base = CONF_HALO - (conf_taps - 1)
    for c in range(conf_w // LANE):
        lanes = slice(c * LANE, (c + 1) * LANE)
        tick()
        acc = None
        for r in range(SUBLANE):
            offs = [o for o in range(base, CONF_HALO + 1) if o % SUBLANE == r]
            if not offs:
                continue
            part = None
            for o in offs:
                k = o - base
                term = wcc_ref[k:k + 1, lanes] * conf_ext[o:o + tt, lanes]
                part = term if part is None else part + term
            acc = part if acc is None else acc + part
        vc = acc + bcc_ref[:, lanes]
        pace = jnp.dot(vc[0:SUBLANE, :].astype(BF16), wgrp_ref[0].astype(BF16),
                       preferred_element_type=F32)
        vc_buf[0:SUBLANE, lanes] = jnp.where(never, pace, vc[0:SUBLANE, :])
        vc_buf[SUBLANE:, lanes] = vc[SUBLANE:, :]
    tick()
    for r in range(tt // LN_ROWS):
        rows = slice(r * LN_ROWS, (r + 1) * LN_ROWS)
        y = _layer_norm_silu(vc_buf[rows, :], gcf_ref[...], bcf_ref[...])
        f_ref[rows, pool_w + lru_w:pool_w + lru_w + conf_w] = y.astype(BF16)
    vlast_ref[...] = conf_ext[tt:tt + CONF_HALO, :]
    conf_ext[0:CONF_HALO, :] = conf_ext[tt:tt + CONF_HALO, :]


def _inmix_kernel(x_ref, g_ref, w_ref, *refs, n_tiles, steps_per_seq, mix):
    n_w = 13
    weights = refs[:n_w]
    xn_ref, f_ref, plast_ref, llast_ref, hlast_ref, vlast_ref = refs[n_w:n_w + 6]
    z_a, z_b, pool_ext, lru_ext, conf_ext, vc_buf, h_carry = refs[n_w + 6:]
    s = pl.program_id(0)
    t_prev = lax.rem(s + (steps_per_seq - 1), steps_per_seq)

    @pl.when(s == 0)
    def _():
        z_b[...] = jnp.zeros_like(z_b)
        _mix_reset(pool_ext, lru_ext, conf_ext, h_carry)

    @pl.when(jnp.logical_and(s > 0, t_prev == 0))
    def _():
        _mix_reset(pool_ext, lru_ext, conf_ext, h_carry)

    zc = w_ref.shape[1]
    sections = 1 + mix["lru_w"] // LANE + mix["conf_w"] // LANE + 1
    cw = zc // sections

    def step(z_w, z_r):
        xn_ref[...] = _rms_norm(x_ref[...], g_ref[...]).astype(BF16)
        done = [0]

        def tick():
            c0 = done[0] * cw
            z_w[:, c0:c0 + cw] = jnp.dot(xn_ref[...], w_ref[:, c0:c0 + cw],
                                         preferred_element_type=F32)
            done[0] += 1

        _mix_tile(t_prev, z_r, *weights, f_ref, plast_ref, llast_ref, hlast_ref, vlast_ref,
                  pool_ext, lru_ext, conf_ext, vc_buf, h_carry, tick=tick, **mix)
        assert done[0] * cw == zc

    @pl.when(lax.rem(s, 2) == 0)
    def _():
        step(z_a, z_b)

    @pl.when(lax.rem(s, 2) == 1)
    def _():
        step(z_b, z_a)


def _inmix(x, g, w_bf, wgrp, pscale, wlc, blc, wa, wx, ba, bx, lam, wcc, bcc, gcf, bcf, l,
           *, batch, seq, rows_p, tt, pool_w, lru_w, conf_w):
    nt, tm, d = x.shape
    zc = w_bf.shape[2]
    steps_per_tile = rows_p // tt
    steps_per_seq = seq // tt
    n_tiles = batch * steps_per_seq
    fw = pool_w + lru_w + conf_w
    mix = dict(tt=tt, pool_w=pool_w, lru_w=lru_w, conf_w=conf_w,
               lru_taps=wlc.shape[1], conf_taps=wcc.shape[1])

    def cur(s):
        c = jnp.minimum(s, n_tiles - 1)
        return (c // steps_per_tile, c % steps_per_tile, 0)

    def prev(s):
        p = jnp.maximum(s - 1, 0)
        return (p // steps_per_tile, p % steps_per_tile, 0)

    smap = lambda s: (jnp.maximum(s - 1, 0) // steps_per_seq, 0, 0)
    vmem = (d * zc * 2 + 2 * tt * zc * 4 + 2 * tt * d * 4 + 2 * tt * d * 2 + 2 * tt * fw * 2
            + (POOL_HALO + tt) * pool_w * 4 + (LRU_HALO + tt) * lru_w * 4
            + (CONF_HALO + tt) * conf_w * 4 + tt * conf_w * 4 + (10 << 20))
    weights = (wgrp, pscale, wlc, blc, wa, wx, ba, bx, lam, wcc, bcc, gcf, bcf)
    return pl.pallas_call(
        functools.partial(_inmix_kernel, n_tiles=n_tiles, steps_per_seq=steps_per_seq, mix=mix),
        grid=(n_tiles + 1,),
        in_specs=[pl.BlockSpec((None, tt, d), cur),
                  _layer_spec(g, l),
                  pl.BlockSpec((None, d, zc), lambda s: (l, 0, 0), pipeline_mode=pl.Buffered(1))]
                 + [_layer_spec(a, l) for a in weights],
        out_specs=[
            pl.BlockSpec((None, tt, d), cur),
            pl.BlockSpec((None, tt, fw), prev),
            pl.BlockSpec((None, POOL_HALO, pool_w), smap),
            pl.BlockSpec((None, LRU_HALO, lru_w), smap),
            pl.BlockSpec((None, SUBLANE, lru_w), smap),
            pl.BlockSpec((None, CONF_HALO, conf_w), smap),
        ],
        out_shape=[
            jax.ShapeDtypeStruct((nt, tm, d), BF16),
            jax.ShapeDtypeStruct((nt, tm, fw), BF16),
            jax.ShapeDtypeStruct((batch, POOL_HALO, pool_w), F32),
            jax.ShapeDtypeStruct((batch, LRU_HALO, lru_w), F32),
            jax.ShapeDtypeStruct((batch, SUBLANE, lru_w), F32),
            jax.ShapeDtypeStruct((batch, CONF_HALO, conf_w), F32),
        ],
        scratch_shapes=[
            pltpu.VMEM((tt, zc), F32),
            pltpu.VMEM((tt, zc), F32),
            pltpu.VMEM((POOL_HALO + tt, pool_w), F32),
            pltpu.VMEM((LRU_HALO + tt, lru_w), F32),
            pltpu.VMEM((CONF_HALO + tt, conf_w), F32),
            pltpu.VMEM((tt, conf_w), F32),
            pltpu.VMEM((SUBLANE, lru_w), F32),
        ],
        compiler_params=_params(vmem, 1),
        name="inmix",
    )(x, g, w_bf, *weights)


def _inproj_sample_kernel(xn_any, x_ref, g_ref, w_ref, xn_ref, zs_ref, xs_buf):
    del xn_any
    i = pl.program_id(0)
    rs = x_ref.shape[0]
    xn = _rms_norm(x_ref[...], g_ref[...]).astype(BF16)
    xn_ref[...] = xn
    xs_buf[pl.ds(pl.multiple_of(i * rs, rs), rs), :] = xn

    @pl.when(i == pl.num_programs(0) - 1)
    def _():
        zs_ref[...] = jnp.dot(xs_buf[...], w_ref[...], preferred_element_type=F32)


def _inproj_sample(xn, x, g, w_bf, l, *, rows_p):
    nt, tm, d = x.shape
    rs = tm - rows_p
    zc = w_bf.shape[2]
    rowmap = lambda i: (i, rows_p // rs, 0)
    vmem = d * zc * 2 + 2 * nt * rs * zc * 4 + nt * rs * d * 2 + (4 << 20)
    return pl.pallas_call(
        _inproj_sample_kernel,
        grid=(nt,),
        in_specs=[pl.BlockSpec(memory_space=pl.ANY),
                  pl.BlockSpec((None, rs, d), rowmap),
                  _layer_spec(g, l),
                  pl.BlockSpec((None, d, zc), lambda i: (l, 0, 0), pipeline_mode=pl.Buffered(1))],
        out_specs=[pl.BlockSpec((None, rs, d), rowmap),
                   pl.BlockSpec((nt * rs, zc), lambda i: (0, 0))],
        out_shape=[jax.ShapeDtypeStruct(xn.shape, xn.dtype),
                   jax.ShapeDtypeStruct((nt * rs, zc), F32)],
        input_output_aliases={0: 0},
        scratch_shapes=[pltpu.VMEM((nt * rs, d), BF16)],
        compiler_params=_params(vmem, 1),
        name="inproj_sample",
    )(xn, x, g, w_bf)


def _mixer_sample_kernel(f_any, z_ref, spool_ref, slru_ref, sh_ref, sconf_ref,
                         wgrp_ref, pscale_ref, wlc_ref, blc_ref, wa_ref, wx_ref, ba_ref,
                         bx_ref, lam_ref, wcc_ref, bcc_ref, gcf_ref, bcf_ref,
                         f_ref, npool_ref, nlru_ref, hnew_ref, nconf_ref, vc_buf,
                         *, pool_w, lru_w, conf_w, lru_taps, conf_taps, pool_buf, start_pos):
    del f_any
    c_lru = pool_w
    c_gel = c_lru + lru_w
    c_ca = c_gel + lru_w
    c_cb = c_ca + conf_w

    for g, w in enumerate(POOL_WINDOWS):
        lanes = slice(g * LANE, (g + 1) * LANE)
        u = z_ref[:, lanes]
        s = u
        for j in range(1, w):
            trow = pool_buf - j
            s = s + spool_ref[:, trow * pool_w + g * LANE:trow * pool_w + (g + 1) * LANE]
        cnt = float(min(w, start_pos + 1))
        d = s / cnt - u
        y = jnp.dot(d.astype(BF16), wgrp_ref[g].astype(BF16), preferred_element_type=F32)
        f_ref[:, lanes] = (y * pscale_ref[:, lanes]).astype(BF16)
    npool_ref[:, 0:(pool_buf - 1) * pool_w] = spool_ref[:, pool_w:]
    npool_ref[:, (pool_buf - 1) * pool_w:] = z_ref[:, 0:pool_w]

    sp_all = _softplus(-lam_ref[...])
    for n in range(lru_w // LANE):
        lanes = slice(n * LANE, (n + 1) * LANE)
        xc = blc_ref[:, lanes] + wlc_ref[lru_taps - 1:lru_taps, lanes] * z_ref[:, c_lru + n * LANE:c_lru + (n + 1) * LANE]
        for k in range(lru_taps - 1):
            xc = xc + wlc_ref[k:k + 1, lanes] * slru_ref[:, k * lru_w + n * LANE:k * lru_w + (n + 1) * LANE]
        xcb = xc.astype(BF16)
        ri = jnp.concatenate(
            [jnp.dot(xcb, wa_ref[n].astype(BF16), preferred_element_type=F32),
             jnp.dot(xcb, wx_ref[n].astype(BF16), preferred_element_type=F32)], axis=1)
        a, gate_i, mult = _lru_coeffs(ri, ba_ref[:, lanes], bx_ref[:, lanes], sp_all[:, lanes])
        if start_pos == 0:
            a = jnp.zeros_like(a)
            mult = jnp.ones_like(mult)
        h = a * sh_ref[:, lanes] + xc * gate_i * mult
        hnew_ref[:, lanes] = h
        gel = _gelu_tanh(z_ref[:, c_gel + n * LANE:c_gel + (n + 1) * LANE])
        f_ref[:, pool_w + n * LANE:pool_w + (n + 1) * LANE] = (h * gel).astype(BF16)
    nlru_ref[:, 0:(lru_taps - 2) * lru_w] = slru_ref[:, lru_w:]
    nlru_ref[:, (lru_taps - 2) * lru_w:] = z_ref[:, c_lru:c_lru + lru_w]

    for c in range(conf_w // LANE):
        lanes = slice(c * LANE, (c + 1) * LANE)
        v = z_ref[:, c_ca + c * LANE:c_ca + (c + 1) * LANE] * _sigmoid(
            z_ref[:, c_cb + c * LANE:c_cb + (c + 1) * LANE])
        nconf_ref[:, (conf_taps - 2) * conf_w + c * LANE:(conf_taps - 2) * conf_w + (c + 1) * LANE] = v
        acc = bcc_ref[:, lanes] + wcc_ref[conf_taps - 1:conf_taps, lanes] * v
        for k in range(conf_taps - 1):
            acc = acc + wcc_ref[k:k + 1, lanes] * sconf_ref[:, k * conf_w + c * LANE:k * conf_w + (c + 1) * LANE]
        vc_buf[:, lanes] = acc
    nconf_ref[:, 0:(conf_taps - 2) * conf_w] = sconf_ref[:, conf_w:]
    y = _layer_norm_silu(vc_buf[...], gcf_ref[...], bcf_ref[...])
    f_ref[:, pool_w + lru_w:pool_w + lru_w + conf_w] = y.astype(BF16)


def _mixer_sample(f, z, spool, slru, sh, sconf, wgrp, pscale, wlc, blc, wa, wx, ba, bx, lam,
                  wcc, bcc, gcf, bcf, l, *, rows_p, pool_w, lru_w, conf_w, start_pos):
    nt, tm, _ = f.shape
    zc = z.shape[1]
    rs = tm - rows_p
    fw = pool_w + lru_w + conf_w
    lru_taps = wlc.shape[1]
    conf_taps = wcc.shape[1]
    pool_buf = spool.shape[2] // pool_w
    rowmap = lambda i: (i, rows_p // rs, 0)
    bmap = lambda i: (i, 0)
    states = (spool, slru, sh, sconf)
    weights = (wgrp, pscale, wlc, blc, wa, wx, ba, bx, lam, wcc, bcc, gcf, bcf)
    vmem = (4 * rs * sum(s.shape[2] for s in states) * 4 * 2
            + 2 * sum(a[0].size * a.dtype.itemsize for a in weights) + (8 << 20))
    kern = functools.partial(_mixer_sample_kernel, pool_w=pool_w, lru_w=lru_w, conf_w=conf_w,
                             lru_taps=lru_taps, conf_taps=conf_taps, pool_buf=pool_buf,
                             start_pos=start_pos)
    return pl.pallas_call(
        kern,
        grid=(nt,),
        in_specs=[pl.BlockSpec(memory_space=pl.ANY),
                  pl.BlockSpec((rs, zc), bmap)]
                 + [pl.BlockSpec((None, rs, s.shape[2]), lambda i: (l, i, 0)) for s in states]
                 + [_layer_spec(a, l) for a in weights],
        out_specs=[pl.BlockSpec((None, rs, fw), rowmap)]
                  + [pl.BlockSpec((rs, s.shape[2]), bmap) for s in states],
        out_shape=[jax.ShapeDtypeStruct(f.shape, f.dtype)]
                  + [jax.ShapeDtypeStruct(s.shape[1:], F32) for s in states],
        input_output_aliases={0: 0},
        scratch_shapes=[pltpu.VMEM((rs, conf_w), F32)],
        compiler_params=_params(vmem, 1),
        name="mixer_sample",
    )(f, z, *states, *weights)


def _gate_merge_kernel(xn_ref, f_ref, wga_ref, wgb_ref, wgc_ref, bga_ref, bgb_ref, bgc_ref,
                       wp_ref, wl_ref, wc_ref, o_ref, wg_bf, wb_bf, *, pool_w, lru_w):
    c1 = pool_w
    c2 = pool_w + lru_w

    @pl.when(pl.program_id(1) == 0)
    def _():
        wg_bf[0] = wga_ref[...].astype(BF16)
        wg_bf[1] = wgb_ref[...].astype(BF16)
        wg_bf[2] = wgc_ref[...].astype(BF16)
        wb_bf[0:c1, :] = wp_ref[...].astype(BF16)
        wb_bf[c1:c2, :] = wl_ref[...].astype(BF16)
        wb_bf[c2:, :] = wc_ref[...].astype(BF16)

    xn = xn_ref[...]
    ga = _sigmoid(jnp.dot(xn, wg_bf[0], preferred_element_type=F32) + bga_ref[...])
    m = ga * jnp.dot(f_ref[:, 0:c1], wb_bf[0:c1, :], preferred_element_type=F32)
    gb = _sigmoid(jnp.dot(xn, wg_bf[1], preferred_element_type=F32) + bgb_ref[...])
    m = m + gb * jnp.dot(f_ref[:, c1:c2], wb_bf[c1:c2, :], preferred_element_type=F32)
    gc = _sigmoid(jnp.dot(xn, wg_bf[2], preferred_element_type=F32) + bgc_ref[...])
    m = m + gc * jnp.dot(f_ref[:, c2:], wb_bf[c2:, :], preferred_element_type=F32)
    o_ref[...] = m.astype(BF16)


def _gate_merge(xn, f, wgate, bgate, wp, wl, wc, l, *, tn):
    nt, tm, d = xn.shape
    fw = f.shape[2]
    pool_w, lru_w = wp.shape[1], wl.shape[1]
    nd = d // tn
    vmem = (2 * tm * d * 2 + 2 * tm * fw * 2 + 2 * (3 * d + fw) * tn * 4 + (3 * d + fw) * tn * 2
            + 2 * tm * tn * 2 + 6 * tm * tn * 4)
    return pl.pallas_call(
        functools.partial(_gate_merge_kernel, pool_w=pool_w, lru_w=lru_w),
        grid=(nd, nt),
        in_specs=[
            pl.BlockSpec((None, tm, d), lambda j, i: (i, 0, 0)),
            pl.BlockSpec((None, tm, fw), lambda j, i: (i, 0, 0)),
            pl.BlockSpec((None, d, tn), lambda j, i: (l, 0, j)),
            pl.BlockSpec((None, d, tn), lambda j, i: (l, 0, nd + j)),
            pl.BlockSpec((None, d, tn), lambda j, i: (l, 0, 2 * nd + j)),
            pl.BlockSpec((None, 1, tn), lambda j, i: (l, 0, j)),
            pl.BlockSpec((None, 1, tn), lambda j, i: (l, 0, nd + j)),
            pl.BlockSpec((None, 1, tn), lambda j, i: (l, 0, 2 * nd + j)),
            pl.BlockSpec((None, pool_w, tn), lambda j, i: (l, 0, j)),
            pl.BlockSpec((None, lru_w, tn), lambda j, i: (l, 0, j)),
            pl.BlockSpec((None, fw - pool_w - lru_w, tn), lambda j, i: (l, 0, j)),
        ],
        out_specs=pl.BlockSpec((None, tm, tn), lambda j, i: (i, 0, j)),
        out_shape=jax.ShapeDtypeStruct((nt, tm, d), BF16),
        scratch_shapes=[pltpu.VMEM((3, d, tn), BF16), pltpu.VMEM((fw, tn), BF16)],
        compiler_params=_params(vmem, 2),
        name="gate_merge",
    )(xn, f, wgate, wgate, wgate, bgate, bgate, bgate, wp, wl, wc)


def _outproj_kernel(m_ref, w_ref, x_ref, o_ref, w_bf):
    @pl.when(pl.program_id(1) == 0)
    def _():
        w_bf[...] = w_ref[...].astype(BF16)

    o_ref[...] = x_ref[...] + jnp.dot(m_ref[...], w_bf[...], preferred_element_type=F32)


def _outproj(m, w, x, l, *, tn):
    nt, tm, d = x.shape
    k = m.shape[2]
    vmem = 2 * tm * k * 2 + 2 * k * tn * 4 + k * tn * 2 + 4 * tm * tn * 4 + tm * tn * 4
    return pl.pallas_call(
        _outproj_kernel,
        grid=(d // tn, nt),
        in_specs=[
            pl.BlockSpec((None, tm, k), lambda j, i: (i, 0, 0)),
            pl.BlockSpec((None, k, tn), lambda j, i: (l, 0, j)),
            pl.BlockSpec((None, tm, tn), lambda j, i: (i, 0, j)),
        ],
        out_specs=pl.BlockSpec((None, tm, tn), lambda j, i: (i, 0, j)),
        out_shape=jax.ShapeDtypeStruct((nt, tm, d), F32),
        scratch_shapes=[pltpu.VMEM((k, tn), BF16)],
        compiler_params=_params(vmem, 2),
        name="outproj",
    )(m, w, x)


def _mlp_kernel(x_ref, g_ref, wup_ref, wdn_ref, o_ref, xn_ref):
    @pl.when(pl.program_id(1) == 0)
    def _():
        x = x_ref[...]
        xn_ref[...] = _rms_norm(x, g_ref[...]).astype(BF16)
        o_ref[...] = x

    hid = jnp.dot(xn_ref[...], wup_ref[...].astype(BF16), preferred_element_type=F32)
    hid = jnp.maximum(hid, 0.0)
    hid = (hid * hid).astype(BF16)
    o_ref[...] += jnp.dot(hid, wdn_ref[...].astype(BF16), preferred_element_type=F32)


def _mlp(x, g, wup, wdn, l, *, tf):
    nt, tm, d = x.shape
    dff = wup.shape[2]
    vmem = (tm * d * 4 + 2 * tm * d * 4 + tm * d * 2 + 4 * d * tf * 4 + 2 * d * tf * 2
            + 2 * tm * tf * 4)
    return pl.pallas_call(
        _mlp_kernel,
        grid=(nt, dff // tf),
        in_specs=[
            pl.BlockSpec((None, tm, d), lambda i, j: (i, 0, 0), pipeline_mode=pl.Buffered(1)),
            _layer_spec(g, l),
            pl.BlockSpec((None, d, tf), lambda i, j: (l, 0, j)),
            pl.BlockSpec((None, tf, d), lambda i, j: (l, j, 0)),
        ],
        out_specs=pl.BlockSpec((None, tm, d), lambda i, j: (i, 0, 0)),
        out_shape=jax.ShapeDtypeStruct((nt, tm, d), F32),
        scratch_shapes=[pltpu.VMEM((tm, d), BF16)],
        compiler_params=_params(vmem, 2),
        name="mlp",
    )(x, g, wup, wdn)


def _final_norm_kernel(x_ref, g_ref, yp_ref, ys_ref):
    rp = yp_ref.shape[0]
    y = _rms_norm(x_ref[...], g_ref[...])
    yp_ref[...] = y[0:rp, :]
    ys_ref[...] = y[rp:, :]


def _final_norm(x, g, *, rows_p):
    nt, tm, d = x.shape
    rs = tm - rows_p
    return pl.pallas_call(
        _final_norm_kernel,
        grid=(nt,),
        in_specs=[pl.BlockSpec((None, tm, d), lambda i: (i, 0, 0)),
                  pl.BlockSpec((1, d), lambda i: (0, 0))],
        out_specs=[pl.BlockSpec((rows_p, d), lambda i: (i, 0)),
                   pl.BlockSpec((rs, d), lambda i: (i, 0))],
        out_shape=[jax.ShapeDtypeStruct((nt * rows_p, d), F32),
                   jax.ShapeDtypeStruct((nt * rs, d), F32)],
        compiler_params=_params(5 * tm * d * 4, 1),
        name="final_norm",
    )(x, g)


def kernel(x_prompt, x_sample, state_pool, state_lru_conv, state_lru_h, state_conf_conv, g_mix, w_in, w_pool_grp, pool_scale, w_pool_br, w_lru_conv, b_lru_conv, w_lru_a, b_lru_a, w_lru_x, b_lru_x, lru_lambda, w_lru_br, w_conf_conv, b_conf_conv, g_conf, b_conf, w_conf_br, w_gate, b_gate, w_out, g_mlp, w_up, w_down, g_final):
    batch, seq, d = x_prompt.shape
    nb, dec_seq, _ = x_sample.shape
    assert dec_seq == 1
    depth = w_in.shape[0]
    pool_w = w_pool_br.shape[1]
    lru_w = w_lru_br.shape[1]
    conf_w = w_conf_br.shape[1]
    pool_buf = state_pool.shape[2]
    lru_buf = state_lru_conv.shape[2]
    conf_buf = state_conf_conv.shape[2]
    n_prompt = batch * seq
    rows_p = n_prompt // NT
    widths = dict(pool_w=pool_w, lru_w=lru_w, conf_w=conf_w)

    row = lambda a: a[:, None, :]
    g_mix_r, g_mlp_r, b_gate_r = row(g_mix), row(g_mlp), row(b_gate)
    pscale_r, blc_r, ba_r, bx_r, lam_r = row(pool_scale), row(b_lru_conv), row(b_lru_a), row(b_lru_x), row(lru_lambda)
    bcc_r, gcf_r, bcf_r = row(b_conf_conv), row(g_conf), row(b_conf)
    spool2 = state_pool.reshape(depth, nb, pool_buf * pool_w)
    slru2 = state_lru_conv.reshape(depth, nb, lru_buf * lru_w)
    sconf2 = state_conf_conv.reshape(depth, nb, conf_buf * conf_w)

    x = _entry(x_prompt.reshape(n_prompt, d), x_sample.reshape(nb, d))
    outs = [[] for _ in range(8)]
    mix_w = (w_pool_grp, pscale_r, w_lru_conv, blc_r, w_lru_a, w_lru_x,
             ba_r, bx_r, lam_r, w_conf_conv, bcc_r, gcf_r, bcf_r)
    w_in_bf = _cast_bf16(w_in, rows=256)
    for l in range(depth):
        xn, f, p_last, l_last, h_last, v_last = _inmix(
            x, g_mix_r, w_in_bf, *mix_w, l, batch=batch, seq=seq, rows_p=rows_p, tt=256, **widths)
        xn, zs = _inproj_sample(xn, x, g_mix_r, w_in_bf, l, rows_p=rows_p)
        f, npool, nlru, nh, nconf = _mixer_sample(
            f, zs, spool2, slru2, state_lru_h, sconf2, *mix_w, l,
            rows_p=rows_p, start_pos=PAST_LEN, **widths)
        merged = _gate_merge(xn, f, w_gate, b_gate_r, w_pool_br, w_lru_br, w_conf_br, l, tn=256)
        x = _outproj(merged, w_out, x, l, tn=1024)
        x = _mlp(x, g_mlp_r, w_up, w_down, l, tf=512)

        outs[0].append(p_last[:, POOL_HALO - pool_buf:])
        outs[1].append(l_last[:, LRU_HALO - lru_buf:])
        outs[2].append(h_last[:, SUBLANE - 1])
        outs[3].append(v_last[:, CONF_HALO - conf_buf:])
        outs[4].append(npool.reshape(nb, pool_buf, pool_w))
        outs[5].append(nlru.reshape(nb, lru_buf, lru_w))
        outs[6].append(nh)
        outs[7].append(nconf.reshape(nb, conf_buf, conf_w))

    y_p, y_s = _final_norm(x, g_final[None, :], rows_p=rows_p)
    return (y_p.reshape(batch, seq, d), y_s.reshape(nb, 1, d)) + tuple(jnp.stack(o) for o in outs)
```

```python
import functools
import math

import jax
import jax.numpy as jnp
from jax import lax
from jax.experimental import pallas as pl
from jax.experimental.pallas import tpu as pltpu

F32 = jnp.float32
BF16 = jnp.bfloat16

POOL_WINDOWS = (2, 4, 8, 16)
LRU_C = 8.0
EPS = 1e-6
PAST_LEN = 16384
LANE = 128
SUBLANE = 8
NT = 8
VMEM_SLACK = 8 << 20
VMEM_CAP = 60 << 20


def _sigmoid(x):
    return 0.5 * jnp.tanh(0.5 * x) + 0.5


def _gelu_tanh(x):
    c = math.sqrt(2.0 / math.pi)
    return x * (0.5 * (1.0 + jnp.tanh(c * (x + 0.044715 * (x * x * x)))))


def _softplus(x):
    return jnp.maximum(x, 0.0) + jnp.log1p(jnp.exp(-jnp.abs(x)))


def _rms_norm(x, g):
    ms = jnp.mean(x * x, axis=-1, keepdims=True)
    return (x * lax.rsqrt(ms + EPS)) * g


def _layer_spec(a, l):
    zeros = (0,) * (a.ndim - 1)
    return pl.BlockSpec((None,) + a.shape[1:], lambda *_: (l,) + zeros)


def _params(vmem_bytes, n_axes):
    return pltpu.CompilerParams(
        dimension_semantics=("arbitrary",) * n_axes,
        vmem_limit_bytes=int(min(vmem_bytes + VMEM_SLACK, VMEM_CAP)))


def _entry_kernel(xp_ref, xs_ref, o_ref):
    rp = xp_ref.shape[0]
    o_ref[0:rp, :] = xp_ref[...]
    o_ref[rp:, :] = xs_ref[...]


def _entry(xp, xs):
    d = xp.shape[1]
    rp, rs = xp.shape[0] // NT, xs.shape[0] // NT
    return pl.pallas_call(
        _entry_kernel,
        grid=(NT,),
        in_specs=[pl.BlockSpec((rp, d), lambda i: (i, 0)),
                  pl.BlockSpec((rs, d), lambda i: (i, 0))],
        out_specs=pl.BlockSpec((None, rp + rs, d), lambda i: (i, 0, 0)),
        out_shape=jax.ShapeDtypeStruct((NT, rp + rs, d), F32),
        compiler_params=_params(4 * (rp + rs) * d * 4, 1),
        name="entry",
    )(xp, xs)


def _cast_kernel(w_ref, o_ref):
    o_ref[...] = w_ref[...].astype(BF16)


def _cast_bf16(w, *, rows):
    depth, k, n = w.shape
    return pl.pallas_call(
        _cast_kernel,
        grid=(depth, k // rows),
        in_specs=[pl.BlockSpec((None, rows, n), lambda l, i: (l, i, 0))],
        out_specs=pl.BlockSpec((None, rows, n), lambda l, i: (l, i, 0)),
        out_shape=jax.ShapeDtypeStruct(w.shape, BF16),
        compiler_params=_params(2 * rows * n * 6, 2),
        name="cast_w_in",
    )(w)


def _lru_coeffs(ri, ba, bx, sp):
    gate_r = _sigmoid(ri[:, :LANE] + ba)
    gate_i = _sigmoid(ri[:, LANE:] + bx)
    log_a = (-LRU_C) * gate_r * sp
    a = jnp.exp(log_a)
    th = jnp.tanh(log_a)
    mult = jnp.sqrt((-2.0 * th) / (1.0 - th))
    return a, gate_i, mult


def _layer_norm_silu(vc, g, b):
    mu = jnp.mean(vc, axis=-1, keepdims=True)
    cen = vc - mu
    var = jnp.mean(cen * cen, axis=-1, keepdims=True)
    y = (cen * lax.rsqrt(var + EPS)) * g + b
    return y * _sigmoid(y)


def _scan_tile(a, b, carry, tt):
    groups = tt // SUBLANE
    a3 = a.reshape(groups, SUBLANE, LANE)
    b3 = b.reshape(groups, SUBLANE, LANE)
    sub = lax.broadcasted_iota(jnp.int32, (groups, SUBLANE, LANE), 1)
    for k in (1, 2, 4):
        keep = sub >= k
        b_sh = jnp.where(keep, pltpu.roll(b3, k, 1), 0.0)
        a_sh = jnp.where(keep, pltpu.roll(a3, k, 1), 1.0)
        b3 = b3 + a3 * b_sh
        a3 = a3 * a_sh
    hs = []
    for j in range(groups):
        hj = b3[j] + a3[j] * carry
        hs.append(hj)
        carry = hj[SUBLANE - 1:SUBLANE, :]
    return jnp.concatenate(hs, axis=0)


POOL_HALO = 16
LRU_HALO = 8
CONF_HALO = 32
LN_ROWS = 64


def _mix_reset(pool_ext, lru_ext, conf_ext, h_carry):
    pool_ext[0:POOL_HALO, :] = jnp.zeros((POOL_HALO, pool_ext.shape[1]), F32)
    lru_ext[0:LRU_HALO, :] = jnp.zeros((LRU_HALO, lru_ext.shape[1]), F32)
    conf_ext[0:CONF_HALO, :] = jnp.zeros((CONF_HALO, conf_ext.shape[1]), F32)
    h_carry[...] = jnp.zeros_like(h_carry)


def _mix_tile(t, z_ref, wgrp_ref, pscale_ref, wlc_ref, blc_ref, wa_ref, wx_ref, ba_ref,
              bx_ref, lam_ref, wcc_ref, bcc_ref, gcf_ref, bcf_ref,
              f_ref, plast_ref, llast_ref, hlast_ref, vlast_ref,
              pool_ext, lru_ext, conf_ext, vc_buf, h_carry,
              *, tt, pool_w, lru_w, conf_w, lru_taps, conf_taps, tick):
    c_lru = pool_w
    c_gel = c_lru + lru_w
    c_ca = c_gel + lru_w
    c_cb = c_ca + conf_w

    row = lax.broadcasted_iota(jnp.int32, (tt, LANE), 0)
    pos = row + t * tt
    posf = pos.astype(F32)
    is_first = pos == 0
    never = posf[0:SUBLANE, :] < 0.0

    tick()
    pool_ext[POOL_HALO:POOL_HALO + tt, :] = z_ref[:, 0:pool_w]
    for g, w in enumerate(POOL_WINDOWS):
        lanes = slice(g * LANE, (g + 1) * LANE)
        u = pool_ext[POOL_HALO:POOL_HALO + tt, lanes]
        s = u
        for j in range(1, w):
            s = s + pool_ext[POOL_HALO - j:POOL_HALO - j + tt, lanes]
        cnt = jnp.minimum(jnp.float32(w), posf + 1.0)
        d = s / cnt - u
        y = jnp.dot(d.astype(BF16), wgrp_ref[g].astype(BF16), preferred_element_type=F32)
        f_ref[:, lanes] = (y * pscale_ref[:, lanes]).astype(BF16)
    plast_ref[...] = pool_ext[tt:tt + POOL_HALO, :]
    pool_ext[0:POOL_HALO, :] = pool_ext[tt:tt + POOL_HALO, :]

    lru_ext[LRU_HALO:LRU_HALO + tt, :] = z_ref[:, c_lru:c_lru + lru_w]
    sp_all = _softplus(-lam_ref[...])
    for n in range(lru_w // LANE):
        lanes = slice(n * LANE, (n + 1) * LANE)
        xc = blc_ref[:, lanes] + wlc_ref[lru_taps - 1:lru_taps, lanes] * lru_ext[LRU_HALO:LRU_HALO + tt, lanes]
        for k in range(lru_taps - 1):
            off = LRU_HALO - (lru_taps - 1) + k
            xc = xc + wlc_ref[k:k + 1, lanes] * lru_ext[off:off + tt, lanes]
        xcb = xc.astype(BF16)
        ri = jnp.concatenate(
            [jnp.dot(xcb, wa_ref[n].astype(BF16), preferred_element_type=F32),
             jnp.dot(xcb, wx_ref[n].astype(BF16), preferred_element_type=F32)], axis=1)
        tick()
        a, gate_i, mult = _lru_coeffs(ri, ba_ref[:, lanes], bx_ref[:, lanes], sp_all[:, lanes])
        a = jnp.where(is_first, 0.0, a)
        mult = jnp.where(is_first, 1.0, mult)
        b = xc * gate_i * mult
        h = _scan_tile(a, b, h_carry[SUBLANE - 1:SUBLANE, lanes], tt)
        h_carry[:, lanes] = h[tt - SUBLANE:tt, :]
        hlast_ref[:, lanes] = h[tt - SUBLANE:tt, :]
        gel = _gelu_tanh(z_ref[:, c_gel + n * LANE:c_gel + (n + 1) * LANE])
        f_ref[:, pool_w + n * LANE:pool_w + (n + 1) * LANE] = (h * gel).astype(BF16)
    llast_ref[...] = lru_ext[tt:tt + LRU_HALO, :]
    lru_ext[0:LRU_HALO, :] = lru_ext[tt:tt + LRU_HALO, :]

    conf_ext[CONF_HALO:CONF_HALO + tt, :] = (
        z_ref[:, c_ca:c_ca + conf_w] * _sigmoid(z_ref[:, c_cb:c_cb + conf_w]))
    base = CONF_HALO - (conf_taps - 1)
    for c in range(conf_w // LANE):
        lanes = slice(c * LANE, (c + 1) * LANE)
        tick()
        acc = None
        for r in range(SUBLANE):
            offs = [o for o in range(base, CONF_HALO + 1) if o % SUBLANE == r]
            if not offs:
                continue
            part = None
            for o in offs:
                k = o - base
                term = wcc_ref[k:k + 1, lanes] * conf_ext[o:o + tt, lanes]
                part = term if part is None else part + term
            acc = part if acc is None else acc + part
        vc = acc + bcc_ref[:, lanes]
        pace = jnp.dot(vc[0:SUBLANE, :].astype(BF16), wgrp_ref[0].astype(BF16),
                       preferred_element_type=F32)
        vc_buf[0:SUBLANE, lanes] = jnp.where(never, pace, vc[0:SUBLANE, :])
        vc_buf[SUBLANE:, lanes] = vc[SUBLANE:, :]
    tick()
    for r in range(tt // LN_ROWS):
        rows = slice(r * LN_ROWS, (r + 1) * LN_ROWS)
        y = _layer_norm_silu(vc_buf[rows, :], gcf_ref[...], bcf_ref[...])
        f_ref[rows, pool_w + lru_w:pool_w + lru_w + conf_w] = y.astype(BF16)
    vlast_ref[...] = conf_ext[tt:tt + CONF_HALO, :]
    conf_ext[0:CONF_HALO, :] = conf_ext[tt:tt + CONF_HALO, :]


def _inmix_kernel(x_ref, g_ref, w_ref, *refs, n_tiles, steps_per_seq, mix):
    n_w = 13
    weights = refs[:n_w]
    xn_ref, f_ref, plast_ref, llast_ref, hlast_ref, vlast_ref = refs[n_w:n_w + 6]
    z_a, z_b, pool_ext, lru_ext, conf_ext, vc_buf, h_carry = refs[n_w + 6:]
    s = pl.program_id(0)
    t_prev = lax.rem(s + (steps_per_seq - 1), steps_per_seq)

    @pl.when(s == 0)
    def _():
        z_b[...] = jnp.zeros_like(z_b)
        _mix_reset(pool_ext, lru_ext, conf_ext, h_carry)

    @pl.when(jnp.logical_and(s > 0, t_prev == 0))
    def _():
        _mix_reset(pool_ext, lru_ext, conf_ext, h_carry)

    zc = w_ref.shape[1]
    sections = 1 + mix["lru_w"] // LANE + mix["conf_w"] // LANE + 1
    cw = zc // sections

    def step(z_w, z_r):
        xn_ref[...] = _rms_norm(x_ref[...], g_ref[...]).astype(BF16)
        done = [0]

        def tick():
            c0 = done[0] * cw
            z_w[:, c0:c0 + cw] = jnp.dot(xn_ref[...], w_ref[:, c0:c0 + cw],
                                         preferred_element_type=F32)
            done[0] += 1

        _mix_tile(t_prev, z_r, *weights, f_ref, plast_ref, llast_ref, hlast_ref, vlast_ref,
                  pool_ext, lru_ext, conf_ext, vc_buf, h_carry, tick=tick, **mix)
        assert done[0] * cw == zc

    @pl.when(lax.rem(s, 2) == 0)
    def _():
        step(z_a, z_b)

    @pl.when(lax.rem(s, 2) == 1)
    def _():
        step(z_b, z_a)


def _inmix(x, g, w_bf, wgrp, pscale, wlc, blc, wa, wx, ba, bx, lam, wcc, bcc, gcf, bcf, l,
           *, batch, seq, rows_p, tt, pool_w, lru_w, conf_w):
    nt, tm, d = x.shape
    zc = w_bf.shape[2]
    steps_per_tile = rows_p // tt
    steps_per_seq = seq // tt
    n_tiles = batch * steps_per_seq
    fw = pool_w + lru_w + conf_w
    mix = dict(tt=tt, pool_w=pool_w, lru_w=lru_w, conf_w=conf_w,
               lru_taps=wlc.shape[1], conf_taps=wcc.shape[1])

    def cur(s):
        c = jnp.minimum(s, n_tiles - 1)
        return (c // steps_per_tile, c % steps_per_tile, 0)

    def prev(s):
        p = jnp.maximum(s - 1, 0)
        return (p // steps_per_tile, p % steps_per_tile, 0)

    smap = lambda s: (jnp.maximum(s - 1, 0) // steps_per_seq, 0, 0)
    vmem = (d * zc * 2 + 2 * tt * zc * 4 + 2 * tt * d * 4 + 2 * tt * d * 2 + 2 * tt * fw * 2
            + (POOL_HALO + tt) * pool_w * 4 + (LRU_HALO + tt) * lru_w * 4
            + (CONF_HALO + tt) * conf_w * 4 + tt * conf_w * 4 + (10 << 20))
    weights = (wgrp, pscale, wlc, blc, wa, wx, ba, bx, lam, wcc, bcc, gcf, bcf)
    return pl.pallas_call(
        functools.partial(_inmix_kernel, n_tiles=n_tiles, steps_per_seq=steps_per_seq, mix=mix),
        grid=(n_tiles + 1,),
        in_specs=[pl.BlockSpec((None, tt, d), cur),
                  _layer_spec(g, l),
                  pl.BlockSpec((None, d, zc), lambda s: (l, 0, 0), pipeline_mode=pl.Buffered(1))]
                 + [_layer_spec(a, l) for a in weights],
        out_specs=[
            pl.BlockSpec((None, tt, d), cur),
            pl.BlockSpec((None, tt, fw), prev),
            pl.BlockSpec((None, POOL_HALO, pool_w), smap),
            pl.BlockSpec((None, LRU_HALO, lru_w), smap),
            pl.BlockSpec((None, SUBLANE, lru_w), smap),
            pl.BlockSpec((None, CONF_HALO, conf_w), smap),
        ],
        out_shape=[
            jax.ShapeDtypeStruct((nt, tm, d), BF16),
            jax.ShapeDtypeStruct((nt, tm, fw), BF16),
            jax.ShapeDtypeStruct((batch, POOL_HALO, pool_w), F32),
            jax.ShapeDtypeStruct((batch, LRU_HALO, lru_w), F32),
            jax.ShapeDtypeStruct((batch, SUBLANE, lru_w), F32),
            jax.ShapeDtypeStruct((batch, CONF_HALO, conf_w), F32),
        ],
        scratch_shapes=[
            pltpu.VMEM((tt, zc), F32),
            pltpu.VMEM((tt, zc), F32),
            pltpu.VMEM((POOL_HALO + tt, pool_w), F32),
            pltpu.VMEM((LRU_HALO + tt, lru_w), F32),
            pltpu.VMEM((CONF_HALO + tt, conf_w), F32),
            pltpu.VMEM((tt, conf_w), F32),
            pltpu.VMEM((SUBLANE, lru_w), F32),
        ],
        compiler_params=_params(vmem, 1),
        name="inmix",
    )(x, g, w_bf, *weights)


def _inproj_sample_kernel(xn_any, x_ref, g_ref, w_ref, xn_ref, zs_ref, xs_buf):
    del xn_any
    i = pl.program_id(0)
    rs = x_ref.shape[0]
    xn = _rms_norm(x_ref[...], g_ref[...]).astype(BF16)
    xn_ref[...] = xn
    xs_buf[pl.ds(pl.multiple_of(i * rs, rs), rs), :] = xn

    @pl.when(i == pl.num_programs(0) - 1)
    def _():
        zs_ref[...] = jnp.dot(xs_buf[...], w_ref[...], preferred_element_type=F32)


def _inproj_sample(xn, x, g, w_bf, l, *, rows_p):
    nt, tm, d = x.shape
    rs = tm - rows_p
    zc = w_bf.shape[2]
    rowmap = lambda i: (i, rows_p // rs, 0)
    vmem = d * zc * 2 + 2 * nt * rs * zc * 4 + nt * rs * d * 2 + (4 << 20)
    return pl.pallas_call(
        _inproj_sample_kernel,
        grid=(nt,),
        in_specs=[pl.BlockSpec(memory_space=pl.ANY),
                  pl.BlockSpec((None, rs, d), rowmap),
                  _layer_spec(g, l),
                  pl.BlockSpec((None, d, zc), lambda i: (l, 0, 0), pipeline_mode=pl.Buffered(1))],
        out_specs=[pl.BlockSpec((None, rs, d), rowmap),
                   pl.BlockSpec((nt * rs, zc), lambda i: (0, 0))],
        out_shape=[jax.ShapeDtypeStruct(xn.shape, xn.dtype),
                   jax.ShapeDtypeStruct((nt * rs, zc), F32)],
        input_output_aliases={0: 0},
        scratch_shapes=[pltpu.VMEM((nt * rs, d), BF16)],
        compiler_params=_params(vmem, 1),
        name="inproj_sample",
    )(xn, x, g, w_bf)


def _mixer_sample_kernel(f_any, z_ref, spool_ref, slru_ref, sh_ref, sconf_ref,
                         wgrp_ref, pscale_ref, wlc_ref, blc_ref, wa_ref, wx_ref, ba_ref,
                         bx_ref, lam_ref, wcc_ref, bcc_ref, gcf_ref, bcf_ref, *rest,
                         pool_w, lru_w, conf_w, lru_taps, conf_taps, pool_buf, start_pos):
    f_ref, npool_ref, nlru_ref, hnew_ref, nconf_ref, vc_buf = rest[-6:]
    del f_any
    c_lru = pool_w
    c_gel = c_lru + lru_w
    c_ca = c_gel + lru_w
    c_cb = c_ca + conf_w

    for g, w in enumerate(POOL_WINDOWS):
        lanes = slice(g * LANE, (g + 1) * LANE)
        u = z_ref[:, lanes]
        s = u
        for j in range(1, w):
            trow = pool_buf - j
            s = s + spool_ref[:, trow * pool_w + g * LANE:trow * pool_w + (g + 1) * LANE]
        cnt = float(min(w, start_pos + 1))
        d = s / cnt - u
        y = jnp.dot(d.astype(BF16), wgrp_ref[g].astype(BF16), preferred_element_type=F32)
        f_ref[:, lanes] = (y * pscale_ref[:, lanes]).astype(BF16)
    npool_ref[:, 0:(pool_buf - 1) * pool_w] = spool_ref[:, pool_w:]
    npool_ref[:, (pool_buf - 1) * pool_w:] = z_ref[:, 0:pool_w]

    sp_all = _softplus(-lam_ref[...])
    for n in range(lru_w // LANE):
        lanes = slice(n * LANE, (n + 1) * LANE)
        xc = blc_ref[:, lanes] + wlc_ref[lru_taps - 1:lru_taps, lanes] * z_ref[:, c_lru + n * LANE:c_lru + (n + 1) * LANE]
        for k in range(lru_taps - 1):
            xc = xc + wlc_ref[k:k + 1, lanes] * slru_ref[:, k * lru_w + n * LANE:k * lru_w + (n + 1) * LANE]
        xcb = xc.astype(BF16)
        ri = jnp.concatenate(
            [jnp.dot(xcb, wa_ref[n].astype(BF16), preferred_element_type=F32),
             jnp.dot(xcb, wx_ref[n].astype(BF16), preferred_element_type=F32)], axis=1)
        a, gate_i, mult = _lru_coeffs(ri, ba_ref[:, lanes], bx_ref[:, lanes], sp_all[:, lanes])
        if start_pos == 0:
            a = jnp.zeros_like(a)
            mult = jnp.ones_like(mult)
        h = a * sh_ref[:, lanes] + xc * gate_i * mult
        hnew_ref[:, lanes] = h
        gel = _gelu_tanh(z_ref[:, c_gel + n * LANE:c_gel + (n + 1) * LANE])
        f_ref[:, pool_w + n * LANE:pool_w + (n + 1) * LANE] = (h * gel).astype(BF16)
    nlru_ref[:, 0:(lru_taps - 2) * lru_w] = slru_ref[:, lru_w:]
    nlru_ref[:, (lru_taps - 2) * lru_w:] = z_ref[:, c_lru:c_lru + lru_w]

    for c in range(conf_w // LANE):
        lanes = slice(c * LANE, (c + 1) * LANE)
        v = z_ref[:, c_ca + c * LANE:c_ca + (c + 1) * LANE] * _sigmoid(
            z_ref[:, c_cb + c * LANE:c_cb + (c + 1) * LANE])
        nconf_ref[:, (conf_taps - 2) * conf_w + c * LANE:(conf_taps - 2) * conf_w + (c + 1) * LANE] = v
        acc = bcc_ref[:, lanes] + wcc_ref[conf_taps - 1:conf_taps, lanes] * v
        for k in range(conf_taps - 1):
            acc = acc + wcc_ref[k:k + 1, lanes] * sconf_ref[:, k * conf_w + c * LANE:k * conf_w + (c + 1) * LANE]
        vc_buf[:, lanes] = acc
    nconf_ref[:, 0:(conf_taps - 2) * conf_w] = sconf_ref[:, conf_w:]
    y = _layer_norm_silu(vc_buf[...], gcf_ref[...], bcf_ref[...])
    f_ref[:, pool_w + lru_w:pool_w + lru_w + conf_w] = y.astype(BF16)


def _mixer_sample(f, z, spool, slru, sh, sconf, wgrp, pscale, wlc, blc, wa, wx, ba, bx, lam,
                  wcc, bcc, gcf, bcf, l, new_states, *, rows_p, pool_w, lru_w, conf_w, start_pos):
    nt, tm, _ = f.shape
    zc = z.shape[1]
    rs = tm - rows_p
    fw = pool_w + lru_w + conf_w
    lru_taps = wlc.shape[1]
    conf_taps = wcc.shape[1]
    pool_buf = spool.shape[2] // pool_w
    rowmap = lambda i: (i, rows_p // rs, 0)
    bmap = lambda i: (i, 0)
    states = (spool, slru, sh, sconf)
    weights = (wgrp, pscale, wlc, blc, wa, wx, ba, bx, lam, wcc, bcc, gcf, bcf)
    n_in = 2 + len(states) + len(weights)
    vmem = (4 * rs * sum(s.shape[2] for s in states) * 4 * 2
            + 2 * sum(a[0].size * a.dtype.itemsize for a in weights) + (8 << 20))
    kern = functools.partial(_mixer_sample_kernel, pool_w=pool_w, lru_w=lru_w, conf_w=conf_w,
                             lru_taps=lru_taps, conf_taps=conf_taps, pool_buf=pool_buf,
                             start_pos=start_pos)
    return pl.pallas_call(
        kern,
        grid=(nt,),
        in_specs=[pl.BlockSpec(memory_space=pl.ANY),
                  pl.BlockSpec((rs, zc), bmap)]
                 + [pl.BlockSpec((None, rs, s.shape[2]), lambda i: (l, i, 0)) for s in states]
                 + [_layer_spec(a, l) for a in weights]
                 + [pl.BlockSpec(memory_space=pl.ANY) for _ in new_states],
        out_specs=[pl.BlockSpec((None, rs, fw), rowmap)]
                  + [pl.BlockSpec((None, rs, s.shape[2]), lambda i: (l, i, 0)) for s in states],
        out_shape=[jax.ShapeDtypeStruct(f.shape, f.dtype)]
                  + [jax.ShapeDtypeStruct(s.shape, F32) for s in states],
        input_output_aliases={0: 0, **{n_in + k: 1 + k for k in range(len(new_states))}},
        scratch_shapes=[pltpu.VMEM((rs, conf_w), F32)],
        compiler_params=_params(vmem, 1),
        name="mixer_sample",
    )(f, z, *states, *weights, *new_states)


def _gate_merge_kernel(xn_ref, f_ref, wga_ref, wgb_ref, wgc_ref, bga_ref, bgb_ref, bgc_ref,
                       wp_ref, wl_ref, wc_ref, o_ref, wg_bf, wb_bf, *, pool_w, lru_w):
    c1 = pool_w
    c2 = pool_w + lru_w

    @pl.when(pl.program_id(1) == 0)
    def _():
        wg_bf[0] = wga_ref[...].astype(BF16)
        wg_bf[1] = wgb_ref[...].astype(BF16)
        wg_bf[2] = wgc_ref[...].astype(BF16)
        wb_bf[0:c1, :] = wp_ref[...].astype(BF16)
        wb_bf[c1:c2, :] = wl_ref[...].astype(BF16)
        wb_bf[c2:, :] = wc_ref[...].astype(BF16)

    xn = xn_ref[...]
    ga = _sigmoid(jnp.dot(xn, wg_bf[0], preferred_element_type=F32) + bga_ref[...])
    m = ga * jnp.dot(f_ref[:, 0:c1], wb_bf[0:c1, :], preferred_element_type=F32)
    gb = _sigmoid(jnp.dot(xn, wg_bf[1], preferred_element_type=F32) + bgb_ref[...])
    m = m + gb * jnp.dot(f_ref[:, c1:c2], wb_bf[c1:c2, :], preferred_element_type=F32)
    gc = _sigmoid(jnp.dot(xn, wg_bf[2], preferred_element_type=F32) + bgc_ref[...])
    m = m + gc * jnp.dot(f_ref[:, c2:], wb_bf[c2:, :], preferred_element_type=F32)
    o_ref[...] = m.astype(BF16)


def _gate_merge(xn, f, wgate, bgate, wp, wl, wc, l, *, tn):
    nt, tm, d = xn.shape
    fw = f.shape[2]
    pool_w, lru_w = wp.shape[1], wl.shape[1]
    nd = d // tn
    vmem = (2 * tm * (d + fw) * 2 + 2 * (3 * d + fw) * tn * 4 + (3 * d + fw) * tn * 2
            + 2 * tm * tn * 2 + 6 * tm * tn * 4)
    return pl.pallas_call(
        functools.partial(_gate_merge_kernel, pool_w=pool_w, lru_w=lru_w),
        grid=(nd, nt),
        in_specs=[
            pl.BlockSpec((None, tm, d), lambda j, i: (i, 0, 0)),
            pl.BlockSpec((None, tm, fw), lambda j, i: (i, 0, 0)),
            pl.BlockSpec((None, d, tn), lambda j, i: (l, 0, j)),
            pl.BlockSpec((None, d, tn), lambda j, i: (l, 0, nd + j)),
            pl.BlockSpec((None, d, tn), lambda j, i: (l, 0, 2 * nd + j)),
            pl.BlockSpec((None, 1, tn), lambda j, i: (l, 0, j)),
            pl.BlockSpec((None, 1, tn), lambda j, i: (l, 0, nd + j)),
            pl.BlockSpec((None, 1, tn), lambda j, i: (l, 0, 2 * nd + j)),
            pl.BlockSpec((None, pool_w, tn), lambda j, i: (l, 0, j)),
            pl.BlockSpec((None, lru_w, tn), lambda j, i: (l, 0, j)),
            pl.BlockSpec((None, fw - pool_w - lru_w, tn), lambda j, i: (l, 0, j)),
        ],
        out_specs=pl.BlockSpec((None, tm, tn), lambda j, i: (i, 0, j)),
        out_shape=jax.ShapeDtypeStruct((nt, tm, d), BF16),
        scratch_shapes=[pltpu.VMEM((3, d, tn), BF16), pltpu.VMEM((fw, tn), BF16)],
        compiler_params=_params(vmem, 2),
        name="gate_merge",
    )(xn, f, wgate, wgate, wgate, bgate, bgate, bgate, wp, wl, wc)


def _outproj_kernel(m_ref, w_ref, x_ref, o_ref, w_bf):
    @pl.when(pl.program_id(1) == 0)
    def _():
        w_bf[...] = w_ref[...].astype(BF16)

    o_ref[...] = x_ref[...] + jnp.dot(m_ref[...], w_bf[...], preferred_element_type=F32)


def _outproj(m, w, x, l, *, tn):
    nt, tm, d = x.shape
    k = m.shape[2]
    vmem = 2 * tm * k * 2 + 2 * k * tn * 4 + k * tn * 2 + 4 * tm * tn * 4 + tm * tn * 4
    return pl.pallas_call(
        _outproj_kernel,
        grid=(d // tn, nt),
        in_specs=[
            pl.BlockSpec((None, tm, k), lambda j, i: (i, 0, 0)),
            pl.BlockSpec((None, k, tn), lambda j, i: (l, 0, j)),
            pl.BlockSpec((None, tm, tn), lambda j, i: (i, 0, j)),
        ],
        out_specs=pl.BlockSpec((None, tm, tn), lambda j, i: (i, 0, j)),
        out_shape=jax.ShapeDtypeStruct((nt, tm, d), F32),
        scratch_shapes=[pltpu.VMEM((k, tn), BF16)],
        compiler_params=_params(vmem, 2),
        name="outproj",
    )(m, w, x)


def _mlp_kernel(x_ref, g_ref, wup_ref, wdn_ref, o_ref, xn_ref):
    @pl.when(pl.program_id(1) == 0)
    def _():
        x = x_ref[...]
        xn_ref[...] = _rms_norm(x, g_ref[...]).astype(BF16)
        o_ref[...] = x

    hid = jnp.dot(xn_ref[...], wup_ref[...].astype(BF16), preferred_element_type=F32)
    hid = jnp.maximum(hid, 0.0)
    hid = (hid * hid).astype(BF16)
    o_ref[...] += jnp.dot(hid, wdn_ref[...].astype(BF16), preferred_element_type=F32)


def _mlp(x, g, wup, wdn, l, *, tf):
    nt, tm, d = x.shape
    dff = wup.shape[2]
    vmem = (tm * d * 4 + 2 * tm * d * 4 + tm * d * 2 + 4 * d * tf * 4 + 2 * d * tf * 2
            + 2 * tm * tf * 4)
    return pl.pallas_call(
        _mlp_kernel,
        grid=(nt, dff // tf),
        in_specs=[
            pl.BlockSpec((None, tm, d), lambda i, j: (i, 0, 0), pipeline_mode=pl.Buffered(1)),
            _layer_spec(g, l),
            pl.BlockSpec((None, d, tf), lambda i, j: (l, 0, j)),
            pl.BlockSpec((None, tf, d), lambda i, j: (l, j, 0)),
        ],
        out_specs=pl.BlockSpec((None, tm, d), lambda i, j: (i, 0, 0)),
        out_shape=jax.ShapeDtypeStruct((nt, tm, d), F32),
        scratch_shapes=[pltpu.VMEM((tm, d), BF16)],
        compiler_params=_params(vmem, 2),
        name="mlp",
    )(x, g, wup, wdn)


def _final_norm_kernel(x_ref, g_ref, yp_ref, ys_ref):
    rp = yp_ref.shape[0]
    y = _rms_norm(x_ref[...], g_ref[...])
    yp_ref[...] = y[0:rp, :]
    ys_ref[...] = y[rp:, :]


def _final_norm(x, g, *, rows_p):
    nt, tm, d = x.shape
    rs = tm - rows_p
    return pl.pallas_call(
        _final_norm_kernel,
        grid=(nt,),
        in_specs=[pl.BlockSpec((None, tm, d), lambda i: (i, 0, 0)),
                  pl.BlockSpec((1, d), lambda i: (0, 0))],
        out_specs=[pl.BlockSpec((rows_p, d), lambda i: (i, 0)),
                   pl.BlockSpec((rs, d), lambda i: (i, 0))],
        out_shape=[jax.ShapeDtypeStruct((nt * rows_p, d), F32),
                   jax.ShapeDtypeStruct((nt * rs, d), F32)],
        compiler_params=_params(5 * tm * d * 4, 1),
        name="final_norm",
    )(x, g)


def kernel(x_prompt, x_sample, state_pool, state_lru_conv, state_lru_h, state_conf_conv, g_mix, w_in, w_pool_grp, pool_scale, w_pool_br, w_lru_conv, b_lru_conv, w_lru_a, b_lru_a, w_lru_x, b_lru_x, lru_lambda, w_lru_br, w_conf_conv, b_conf_conv, g_conf, b_conf, w_conf_br, w_gate, b_gate, w_out, g_mlp, w_up, w_down, g_final):
    batch, seq, d = x_prompt.shape
    nb, dec_seq, _ = x_sample.shape
    assert dec_seq == 1
    depth = w_in.shape[0]
    pool_w = w_pool_br.shape[1]
    lru_w = w_lru_br.shape[1]
    conf_w = w_conf_br.shape[1]
    pool_buf = state_pool.shape[2]
    lru_buf = state_lru_conv.shape[2]
    conf_buf = state_conf_conv.shape[2]
    n_prompt = batch * seq
    rows_p = n_prompt // NT
    widths = dict(pool_w=pool_w, lru_w=lru_w, conf_w=conf_w)

    row = lambda a: a[:, None, :]
    g_mix_r, g_mlp_r, b_gate_r = row(g_mix), row(g_mlp), row(b_gate)
    pscale_r, blc_r, ba_r, bx_r, lam_r = row(pool_scale), row(b_lru_conv), row(b_lru_a), row(b_lru_x), row(lru_lambda)
    bcc_r, gcf_r, bcf_r = row(b_conf_conv), row(g_conf), row(b_conf)
    spool2 = state_pool.reshape(depth, nb, pool_buf * pool_w)
    slru2 = state_lru_conv.reshape(depth, nb, lru_buf * lru_w)
    sconf2 = state_conf_conv.reshape(depth, nb, conf_buf * conf_w)

    x = _entry(x_prompt.reshape(n_prompt, d), x_sample.reshape(nb, d))
    outs = [[] for _ in range(4)]
    new_states = ()
    mix_w = (w_pool_grp, pscale_r, w_lru_conv, blc_r, w_lru_a, w_lru_x,
             ba_r, bx_r, lam_r, w_conf_conv, bcc_r, gcf_r, bcf_r)
    w_in_bf = _cast_bf16(w_in, rows=256)
    for l in range(depth):
        xn, f, p_last, l_last, h_last, v_last = _inmix(
            x, g_mix_r, w_in_bf, *mix_w, l, batch=batch, seq=seq, rows_p=rows_p, tt=256, **widths)
        xn, zs = _inproj_sample(xn, x, g_mix_r, w_in_bf, l, rows_p=rows_p)
        f, *new_states = _mixer_sample(
            f, zs, spool2, slru2, state_lru_h, sconf2, *mix_w, l, tuple(new_states),
            rows_p=rows_p, start_pos=PAST_LEN, **widths)
        merged = _gate_merge(xn, f, w_gate, b_gate_r, w_pool_br, w_lru_br, w_conf_br, l, tn=256)
        x = _outproj(merged, w_out, x, l, tn=1024)
        x = _mlp(x, g_mlp_r, w_up, w_down, l, tf=512)

        outs[0].append(p_last[:, POOL_HALO - pool_buf:])
        outs[1].append(l_last[:, LRU_HALO - lru_buf:])
        outs[2].append(h_last[:, SUBLANE - 1])
        outs[3].append(v_last[:, CONF_HALO - conf_buf:])

    npool, nlru, nh, nconf = new_states
    y_p, y_s = _final_norm(x, g_final[None, :], rows_p=rows_p)
    return ((y_p.reshape(batch, seq, d), y_s.reshape(nb, 1, d))
            + tuple(jnp.stack(o) for o in outs)
            + (npool.reshape(depth, nb, pool_buf, pool_w), nlru.reshape(depth, nb, lru_buf, lru_w),
               nh, nconf.reshape(depth, nb, conf_buf, conf_w)))
```

```python
import functools
import math

import jax
import jax.numpy as jnp
from jax import lax
from jax.experimental import pallas as pl
from jax.experimental.pallas import tpu as pltpu

F32 = jnp.float32
BF16 = jnp.bfloat16

POOL_WINDOWS = (2, 4, 8, 16)
LRU_C = 8.0
EPS = 1e-6
PAST_LEN = 16384
LANE = 128
SUBLANE = 8
NT = 8
VMEM_SLACK = 8 << 20
VMEM_CAP = 60 << 20


def _sigmoid(x):
    return 0.5 * jnp.tanh(0.5 * x) + 0.5


def _gelu_tanh(x):
    c = math.sqrt(2.0 / math.pi)
    return x * (0.5 * (1.0 + jnp.tanh(c * (x + 0.044715 * (x * x * x)))))


def _softplus(x):
    return jnp.maximum(x, 0.0) + jnp.log1p(jnp.exp(-jnp.abs(x)))


def _rms_norm(x, g):
    ms = jnp.mean(x * x, axis=-1, keepdims=True)
    return (x * lax.rsqrt(ms + EPS)) * g


def _layer_spec(a, l):
    zeros = (0,) * (a.ndim - 1)
    return pl.BlockSpec((None,) + a.shape[1:], lambda *_: (l,) + zeros)


def _params(vmem_bytes, n_axes):
    return pltpu.CompilerParams(
        dimension_semantics=("arbitrary",) * n_axes,
        vmem_limit_bytes=int(min(vmem_bytes + VMEM_SLACK, VMEM_CAP)))


def _cast_kernel(w_ref, o_ref):
    o_ref[...] = w_ref[...].astype(BF16)


def _cast_bf16(w, *, rows):
    depth, k, n = w.shape
    return pl.pallas_call(
        _cast_kernel,
        grid=(depth, k // rows),
        in_specs=[pl.BlockSpec((None, rows, n), lambda l, i: (l, i, 0))],
        out_specs=pl.BlockSpec((None, rows, n), lambda l, i: (l, i, 0)),
        out_shape=jax.ShapeDtypeStruct(w.shape, BF16),
        compiler_params=_params(2 * rows * n * 6, 2),
        name="cast_w_in",
    )(w)


def _lru_coeffs(ri, ba, bx, sp):
    gate_r = _sigmoid(ri[:, :LANE] + ba)
    gate_i = _sigmoid(ri[:, LANE:] + bx)
    log_a = (-LRU_C) * gate_r * sp
    a = jnp.exp(log_a)
    th = jnp.tanh(log_a)
    mult = jnp.sqrt((-2.0 * th) / (1.0 - th))
    return a, gate_i, mult


def _layer_norm_silu(vc, g, b):
    mu = jnp.mean(vc, axis=-1, keepdims=True)
    cen = vc - mu
    var = jnp.mean(cen * cen, axis=-1, keepdims=True)
    y = (cen * lax.rsqrt(var + EPS)) * g + b
    return y * _sigmoid(y)


def _scan_tile(a, b, carry, tt):
    groups = tt // SUBLANE
    a3 = a.reshape(groups, SUBLANE, LANE)
    b3 = b.reshape(groups, SUBLANE, LANE)
    sub = lax.broadcasted_iota(jnp.int32, (groups, SUBLANE, LANE), 1)
    for k in (1, 2, 4):
        keep = sub >= k
        b_sh = jnp.where(keep, pltpu.roll(b3, k, 1), 0.0)
        a_sh = jnp.where(keep, pltpu.roll(a3, k, 1), 1.0)
        b3 = b3 + a3 * b_sh
        a3 = a3 * a_sh
    hs = []
    for j in range(groups):
        hj = b3[j] + a3[j] * carry
        hs.append(hj)
        carry = hj[SUBLANE - 1:SUBLANE, :]
    return jnp.concatenate(hs, axis=0)


POOL_HALO = 16
LRU_HALO = 8
CONF_HALO = 32
LN_ROWS = 64


def _mix_reset(pool_ext, lru_ext, conf_ext, h_carry):
    pool_ext[0:POOL_HALO, :] = jnp.zeros((POOL_HALO, pool_ext.shape[1]), F32)
    lru_ext[0:LRU_HALO, :] = jnp.zeros((LRU_HALO, lru_ext.shape[1]), F32)
    conf_ext[0:CONF_HALO, :] = jnp.zeros((CONF_HALO, conf_ext.shape[1]), F32)
    h_carry[...] = jnp.zeros_like(h_carry)


def _mix_tile(t, z_ref, wgrp_ref, pscale_ref, wlc_ref, blc_ref, wa_ref, wx_ref, ba_ref,
              bx_ref, lam_ref, wcc_ref, bcc_ref, gcf_ref, bcf_ref,
              f_ref, plast_ref, llast_ref, hlast_ref, vlast_ref,
              pool_ext, lru_ext, conf_ext, vc_buf, h_carry,
              *, tt, pool_w, lru_w, conf_w, lru_taps, conf_taps, tick):
    c_lru = pool_w
    c_gel = c_lru + lru_w
    c_ca = c_gel + lru_w
    c_cb = c_ca + conf_w

    row = lax.broadcasted_iota(jnp.int32, (tt, LANE), 0)
    pos = row + t * tt
    posf = pos.astype(F32)
    is_first = pos == 0
    never = posf[0:SUBLANE, :] < 0.0

    tick()
    pool_ext[POOL_HALO:POOL_HALO + tt, :] = z_ref[:, 0:pool_w]
    for g, w in enumerate(POOL_WINDOWS):
        lanes = slice(g * LANE, (g + 1) * LANE)
        u = pool_ext[POOL_HALO:POOL_HALO + tt, lanes]
        s = u
        for j in range(1, w):
            s = s + pool_ext[POOL_HALO - j:POOL_HALO - j + tt, lanes]
        cnt = jnp.minimum(jnp.float32(w), posf + 1.0)
        d = s / cnt - u
        y = jnp.dot(d.astype(BF16), wgrp_ref[g].astype(BF16), preferred_element_type=F32)
        f_ref[:, lanes] = (y * pscale_ref[:, lanes]).astype(BF16)
    plast_ref[...] = pool_ext[tt:tt + POOL_HALO, :]
    pool_ext[0:POOL_HALO, :] = pool_ext[tt:tt + POOL_HALO, :]

    lru_ext[LRU_HALO:LRU_HALO + tt, :] = z_ref[:, c_lru:c_lru + lru_w]
    sp_all = _softplus(-lam_ref[...])
    for n in range(lru_w // LANE):
        lanes = slice(n * LANE, (n + 1) * LANE)
        xc = blc_ref[:, lanes] + wlc_ref[lru_taps - 1:lru_taps, lanes] * lru_ext[LRU_HALO:LRU_HALO + tt, lanes]
        for k in range(lru_taps - 1):
            off = LRU_HALO - (lru_taps - 1) + k
            xc = xc + wlc_ref[k:k + 1, lanes] * lru_ext[off:off + tt, lanes]
        xcb = xc.astype(BF16)
        ri = jnp.concatenate(
            [jnp.dot(xcb, wa_ref[n].astype(BF16), preferred_element_type=F32),
             jnp.dot(xcb, wx_ref[n].astype(BF16), preferred_element_type=F32)], axis=1)
        tick()
        a, gate_i, mult = _lru_coeffs(ri, ba_ref[:, lanes], bx_ref[:, lanes], sp_all[:, lanes])
        a = jnp.where(is_first, 0.0, a)
        mult = jnp.where(is_first, 1.0, mult)
        b = xc * gate_i * mult
        h = _scan_tile(a, b, h_carry[SUBLANE - 1:SUBLANE, lanes], tt)
        h_carry[:, lanes] = h[tt - SUBLANE:tt, :]
        hlast_ref[:, lanes] = h[tt - SUBLANE:tt, :]
        gel = _gelu_tanh(z_ref[:, c_gel + n * LANE:c_gel + (n + 1) * LANE])
        f_ref[:, pool_w + n * LANE:pool_w + (n + 1) * LANE] = (h * gel).astype(BF16)
    llast_ref[...] = lru_ext[tt:tt + LRU_HALO, :]
    lru_ext[0:LRU_HALO, :] = lru_ext[tt:tt + LRU_HALO, :]

    conf_ext[CONF_HALO:CONF_HALO + tt, :] = (
        z_ref[:, c_ca:c_ca + conf_w] * _sigmoid(z_ref[:, c_cb:c_cb + conf_w]))
    base = CONF_HALO - (conf_taps - 1)
    for c in range(conf_w // LANE):
        lanes = slice(c * LANE, (c + 1) * LANE)
        tick()
        acc = None
        for r in range(SUBLANE):
            offs = [o for o in range(base, CONF_HALO + 1) if o % SUBLANE == r]
            if not offs:
                continue
            part = None
            for o in offs:
                k = o - base
                term = wcc_ref[k:k + 1, lanes] * conf_ext[o:o + tt, lanes]
                part = term if part is None else part + term
            acc = part if acc is None else acc + part
        vc = acc + bcc_ref[:, lanes]
        pace = jnp.dot(vc[0:SUBLANE, :].astype(BF16), wgrp_ref[0].astype(BF16),
                       preferred_element_type=F32)
        vc_buf[0:SUBLANE, lanes] = jnp.where(never, pace, vc[0:SUBLANE, :])
        vc_buf[SUBLANE:, lanes] = vc[SUBLANE:, :]
    tick()
    for r in range(tt // LN_ROWS):
        rows = slice(r * LN_ROWS, (r + 1) * LN_ROWS)
        y = _layer_norm_silu(vc_buf[rows, :], gcf_ref[...], bcf_ref[...])
        f_ref[rows, pool_w + lru_w:pool_w + lru_w + conf_w] = y.astype(BF16)
    vlast_ref[...] = conf_ext[tt:tt + CONF_HALO, :]
    conf_ext[0:CONF_HALO, :] = conf_ext[tt:tt + CONF_HALO, :]


def _inmix_kernel(x_ref, g_ref, w_ref, *refs, n_tiles, steps_per_seq, mix):
    n_w = 13
    weights = refs[:n_w]
    xn_ref, f_ref, plast_ref, llast_ref, hlast_ref, vlast_ref = refs[n_w:n_w + 6]
    z_a, z_b, pool_ext, lru_ext, conf_ext, vc_buf, h_carry = refs[n_w + 6:]
    s = pl.program_id(0)
    t_prev = lax.rem(s + (steps_per_seq - 1), steps_per_seq)

    @pl.when(s == 0)
    def _():
        z_b[...] = jnp.zeros_like(z_b)
        _mix_reset(pool_ext, lru_ext, conf_ext, h_carry)

    @pl.when(jnp.logical_and(s > 0, t_prev == 0))
    def _():
        _mix_reset(pool_ext, lru_ext, conf_ext, h_carry)

    zc = w_ref.shape[1]
    sections = 1 + mix["lru_w"] // LANE + mix["conf_w"] // LANE + 1
    cw = zc // sections

    def step(z_w, z_r):
        xn_ref[...] = _rms_norm(x_ref[...], g_ref[...]).astype(BF16)
        done = [0]

        def tick():
            c0 = done[0] * cw
            z_w[:, c0:c0 + cw] = jnp.dot(xn_ref[...], w_ref[:, c0:c0 + cw],
                                         preferred_element_type=F32)
            done[0] += 1

        _mix_tile(t_prev, z_r, *weights, f_ref, plast_ref, llast_ref, hlast_ref, vlast_ref,
                  pool_ext, lru_ext, conf_ext, vc_buf, h_carry, tick=tick, **mix)
        assert done[0] * cw == zc

    @pl.when(lax.rem(s, 2) == 0)
    def _():
        step(z_a, z_b)

    @pl.when(lax.rem(s, 2) == 1)
    def _():
        step(z_b, z_a)


def _inmix(x, g, w_bf, wgrp, pscale, wlc, blc, wa, wx, ba, bx, lam, wcc, bcc, gcf, bcf, l,
           *, batch, seq, rows_p, tm, tt, pool_w, lru_w, conf_w):
    d = x.shape[-1]
    nt = batch * seq // rows_p
    zc = w_bf.shape[2]
    steps_per_tile = rows_p // tt
    steps_per_seq = seq // tt
    n_tiles = batch * steps_per_seq
    fw = pool_w + lru_w + conf_w
    mix = dict(tt=tt, pool_w=pool_w, lru_w=lru_w, conf_w=conf_w,
               lru_taps=wlc.shape[1], conf_taps=wcc.shape[1])

    def cur(s):
        c = jnp.minimum(s, n_tiles - 1)
        return (c // steps_per_tile, c % steps_per_tile, 0)

    def prev(s):
        p = jnp.maximum(s - 1, 0)
        return (p // steps_per_tile, p % steps_per_tile, 0)

    smap = lambda s: (jnp.maximum(s - 1, 0) // steps_per_seq, 0, 0)
    vmem = (d * zc * 2 + 2 * tt * zc * 4 + 2 * tt * d * 4 + 2 * tt * d * 2 + 2 * tt * fw * 2
            + (POOL_HALO + tt) * pool_w * 4 + (LRU_HALO + tt) * lru_w * 4
            + (CONF_HALO + tt) * conf_w * 4 + tt * conf_w * 4 + (10 << 20))
    weights = (wgrp, pscale, wlc, blc, wa, wx, ba, bx, lam, wcc, bcc, gcf, bcf)
    x_spec = (pl.BlockSpec((tt, d), lambda s: (jnp.minimum(s, n_tiles - 1), 0)) if x.ndim == 2
              else pl.BlockSpec((None, tt, d), cur))
    return pl.pallas_call(
        functools.partial(_inmix_kernel, n_tiles=n_tiles, steps_per_seq=steps_per_seq, mix=mix),
        grid=(n_tiles + 1,),
        in_specs=[x_spec,
                  _layer_spec(g, l),
                  pl.BlockSpec((None, d, zc), lambda s: (l, 0, 0), pipeline_mode=pl.Buffered(1))]
                 + [_layer_spec(a, l) for a in weights],
        out_specs=[
            pl.BlockSpec((None, tt, d), cur),
            pl.BlockSpec((None, tt, fw), prev),
            pl.BlockSpec((None, POOL_HALO, pool_w), smap),
            pl.BlockSpec((None, LRU_HALO, lru_w), smap),
            pl.BlockSpec((None, SUBLANE, lru_w), smap),
            pl.BlockSpec((None, CONF_HALO, conf_w), smap),
        ],
        out_shape=[
            jax.ShapeDtypeStruct((nt, tm, d), BF16),
            jax.ShapeDtypeStruct((nt, tm, fw), BF16),
            jax.ShapeDtypeStruct((batch, POOL_HALO, pool_w), F32),
            jax.ShapeDtypeStruct((batch, LRU_HALO, lru_w), F32),
            jax.ShapeDtypeStruct((batch, SUBLANE, lru_w), F32),
            jax.ShapeDtypeStruct((batch, CONF_HALO, conf_w), F32),
        ],
        scratch_shapes=[
            pltpu.VMEM((tt, zc), F32),
            pltpu.VMEM((tt, zc), F32),
            pltpu.VMEM((POOL_HALO + tt, pool_w), F32),
            pltpu.VMEM((LRU_HALO + tt, lru_w), F32),
            pltpu.VMEM((CONF_HALO + tt, conf_w), F32),
            pltpu.VMEM((tt, conf_w), F32),
            pltpu.VMEM((SUBLANE, lru_w), F32),
        ],
        compiler_params=_params(vmem, 1),
        name="inmix",
    )(x, g, w_bf, *weights)


def _inproj_sample_kernel(xn_any, x_ref, g_ref, w_ref, xn_ref, zs_ref, xs_buf):
    del xn_any
    i = pl.program_id(0)
    rs = x_ref.shape[0]
    xn = _rms_norm(x_ref[...], g_ref[...]).astype(BF16)
    xn_ref[...] = xn
    xs_buf[pl.ds(pl.multiple_of(i * rs, rs), rs), :] = xn

    @pl.when(i == pl.num_programs(0) - 1)
    def _():
        zs_ref[...] = jnp.dot(xs_buf[...], w_ref[...], preferred_element_type=F32)


def _inproj_sample(xn, x, g, w_bf, l, *, rows_p):
    nt, tm, d = xn.shape
    rs = tm - rows_p
    zc = w_bf.shape[2]
    rowmap = lambda i: (i, rows_p // rs, 0)
    x_spec = (pl.BlockSpec((rs, d), lambda i: (i, 0)) if x.ndim == 2
              else pl.BlockSpec((None, rs, d), rowmap))
    vmem = d * zc * 2 + 2 * nt * rs * zc * 4 + nt * rs * d * 2 + (4 << 20)
    return pl.pallas_call(
        _inproj_sample_kernel,
        grid=(nt,),
        in_specs=[pl.BlockSpec(memory_space=pl.ANY),
                  x_spec,
                  _layer_spec(g, l),
                  pl.BlockSpec((None, d, zc), lambda i: (l, 0, 0), pipeline_mode=pl.Buffered(1))],
        out_specs=[pl.BlockSpec((None, rs, d), rowmap),
                   pl.BlockSpec((nt * rs, zc), lambda i: (0, 0))],
        out_shape=[jax.ShapeDtypeStruct(xn.shape, xn.dtype),
                   jax.ShapeDtypeStruct((nt * rs, zc), F32)],
        input_output_aliases={0: 0},
        scratch_shapes=[pltpu.VMEM((nt * rs, d), BF16)],
        compiler_params=_params(vmem, 1),
        name="inproj_sample",
    )(xn, x, g, w_bf)


def _mixer_sample_kernel(f_any, z_ref, spool_ref, slru_ref, sh_ref, sconf_ref,
                         wgrp_ref, pscale_ref, wlc_ref, blc_ref, wa_ref, wx_ref, ba_ref,
                         bx_ref, lam_ref, wcc_ref, bcc_ref, gcf_ref, bcf_ref, *rest,
                         pool_w, lru_w, conf_w, lru_taps, conf_taps, pool_buf, start_pos):
    f_ref, npool_ref, nlru_ref, hnew_ref, nconf_ref, vc_buf = rest[-6:]
    del f_any
    c_lru = pool_w
    c_gel = c_lru + lru_w
    c_ca = c_gel + lru_w
    c_cb = c_ca + conf_w

    for g, w in enumerate(POOL_WINDOWS):
        lanes = slice(g * LANE, (g + 1) * LANE)
        u = z_ref[:, lanes]
        s = u
        for j in range(1, w):
            trow = pool_buf - j
            s = s + spool_ref[:, trow, lanes]
        cnt = float(min(w, start_pos + 1))
        d = s / cnt - u
        y = jnp.dot(d.astype(BF16), wgrp_ref[g].astype(BF16), preferred_element_type=F32)
        f_ref[:, lanes] = (y * pscale_ref[:, lanes]).astype(BF16)
    npool_ref[:, 0:pool_buf - 1, :] = spool_ref[:, 1:pool_buf, :]
    npool_ref[:, pool_buf - 1, :] = z_ref[:, 0:pool_w]

    sp_all = _softplus(-lam_ref[...])
    for n in range(lru_w // LANE):
        lanes = slice(n * LANE, (n + 1) * LANE)
        xc = blc_ref[:, lanes] + wlc_ref[lru_taps - 1:lru_taps, lanes] * z_ref[:, c_lru + n * LANE:c_lru + (n + 1) * LANE]
        for k in range(lru_taps - 1):
            xc = xc + wlc_ref[k:k + 1, lanes] * slru_ref[:, k, lanes]
        xcb = xc.astype(BF16)
        ri = jnp.concatenate(
            [jnp.dot(xcb, wa_ref[n].astype(BF16), preferred_element_type=F32),
             jnp.dot(xcb, wx_ref[n].astype(BF16), preferred_element_type=F32)], axis=1)
        a, gate_i, mult = _lru_coeffs(ri, ba_ref[:, lanes], bx_ref[:, lanes], sp_all[:, lanes])
        if start_pos == 0:
            a = jnp.zeros_like(a)
            mult = jnp.ones_like(mult)
        h = a * sh_ref[:, lanes] + xc * gate_i * mult
        hnew_ref[:, lanes] = h
        gel = _gelu_tanh(z_ref[:, c_gel + n * LANE:c_gel + (n + 1) * LANE])
        f_ref[:, pool_w + n * LANE:pool_w + (n + 1) * LANE] = (h * gel).astype(BF16)
    nlru_ref[:, 0:lru_taps - 2, :] = slru_ref[:, 1:lru_taps - 1, :]
    nlru_ref[:, lru_taps - 2, :] = z_ref[:, c_lru:c_lru + lru_w]

    for c in range(conf_w // LANE):
        lanes = slice(c * LANE, (c + 1) * LANE)
        v = z_ref[:, c_ca + c * LANE:c_ca + (c + 1) * LANE] * _sigmoid(
            z_ref[:, c_cb + c * LANE:c_cb + (c + 1) * LANE])
        nconf_ref[:, conf_taps - 2, lanes] = v
        acc = bcc_ref[:, lanes] + wcc_ref[conf_taps - 1:conf_taps, lanes] * v
        for k in range(conf_taps - 1):
            acc = acc + wcc_ref[k:k + 1, lanes] * sconf_ref[:, k, lanes]
        vc_buf[:, lanes] = acc
    nconf_ref[:, 0:conf_taps - 2, :] = sconf_ref[:, 1:conf_taps - 1, :]
    y = _layer_norm_silu(vc_buf[...], gcf_ref[...], bcf_ref[...])
    f_ref[:, pool_w + lru_w:pool_w + lru_w + conf_w] = y.astype(BF16)


def _mixer_sample(f, z, spool, slru, sh, sconf, wgrp, pscale, wlc, blc, wa, wx, ba, bx, lam,
                  wcc, bcc, gcf, bcf, l, new_states, *, rows_p, pool_w, lru_w, conf_w, start_pos):
    nt, tm, _ = f.shape
    zc = z.shape[1]
    rs = tm - rows_p
    fw = pool_w + lru_w + conf_w
    lru_taps = wlc.shape[1]
    conf_taps = wcc.shape[1]
    pool_buf = spool.shape[2]
    rowmap = lambda i: (i, rows_p // rs, 0)
    bmap = lambda i: (i, 0)
    states = (spool, slru, sh, sconf)
    weights = (wgrp, pscale, wlc, blc, wa, wx, ba, bx, lam, wcc, bcc, gcf, bcf)
    n_in = 2 + len(states) + len(weights)

    def state_spec(s):
        zeros = (0,) * (s.ndim - 2)
        return pl.BlockSpec((None, rs) + s.shape[2:], lambda i: (l, i) + zeros)

    vmem = (4 * rs * sum(math.prod(s.shape[2:]) for s in states) * 4 * 2
            + 2 * sum(a[0].size * a.dtype.itemsize for a in weights) + (8 << 20))
    kern = functools.partial(_mixer_sample_kernel, pool_w=pool_w, lru_w=lru_w, conf_w=conf_w,
                             lru_taps=lru_taps, conf_taps=conf_taps, pool_buf=pool_buf,
                             start_pos=start_pos)
    return pl.pallas_call(
        kern,
        grid=(nt,),
        in_specs=[pl.BlockSpec(memory_space=pl.ANY),
                  pl.BlockSpec((rs, zc), bmap)]
                 + [state_spec(s) for s in states]
                 + [_layer_spec(a, l) for a in weights]
                 + [pl.BlockSpec(memory_space=pl.ANY) for _ in new_states],
        out_specs=[pl.BlockSpec((None, rs, fw), rowmap)] + [state_spec(s) for s in states],
        out_shape=[jax.ShapeDtypeStruct(f.shape, f.dtype)]
                  + [jax.ShapeDtypeStruct(s.shape, F32) for s in states],
        input_output_aliases={0: 0, **{n_in + k: 1 + k for k in range(len(new_states))}},
        scratch_shapes=[pltpu.VMEM((rs, conf_w), F32)],
        compiler_params=_params(vmem, 1),
        name="mixer_sample",
    )(f, z, *states, *weights, *new_states)


def _gate_merge_kernel(xn_ref, f_ref, wga_ref, wgb_ref, wgc_ref, bga_ref, bgb_ref, bgc_ref,
                       wp_ref, wl_ref, wc_ref, o_ref, wg_bf, wb_bf, *, pool_w, lru_w):
    c1 = pool_w
    c2 = pool_w + lru_w

    @pl.when(pl.program_id(1) == 0)
    def _():
        wg_bf[0] = wga_ref[...].astype(BF16)
        wg_bf[1] = wgb_ref[...].astype(BF16)
        wg_bf[2] = wgc_ref[...].astype(BF16)
        wb_bf[0:c1, :] = wp_ref[...].astype(BF16)
        wb_bf[c1:c2, :] = wl_ref[...].astype(BF16)
        wb_bf[c2:, :] = wc_ref[...].astype(BF16)

    xn = xn_ref[...]
    ga = _sigmoid(jnp.dot(xn, wg_bf[0], preferred_element_type=F32) + bga_ref[...])
    m = ga * jnp.dot(f_ref[:, 0:c1], wb_bf[0:c1, :], preferred_element_type=F32)
    gb = _sigmoid(jnp.dot(xn, wg_bf[1], preferred_element_type=F32) + bgb_ref[...])
    m = m + gb * jnp.dot(f_ref[:, c1:c2], wb_bf[c1:c2, :], preferred_element_type=F32)
    gc = _sigmoid(jnp.dot(xn, wg_bf[2], preferred_element_type=F32) + bgc_ref[...])
    m = m + gc * jnp.dot(f_ref[:, c2:], wb_bf[c2:, :], preferred_element_type=F32)
    o_ref[...] = m.astype(BF16)


def _gate_merge(xn, f, wgate, bgate, wp, wl, wc, l, *, tn):
    nt, tm, d = xn.shape
    fw = f.shape[2]
    pool_w, lru_w = wp.shape[1], wl.shape[1]
    nd = d // tn
    vmem = (2 * tm * (d + fw) * 2 + 2 * (3 * d + fw) * tn * 4 + (3 * d + fw) * tn * 2
            + 2 * tm * tn * 2 + 6 * tm * tn * 4)
    return pl.pallas_call(
        functools.partial(_gate_merge_kernel, pool_w=pool_w, lru_w=lru_w),
        grid=(nd, nt),
        in_specs=[
            pl.BlockSpec((None, tm, d), lambda j, i: (i, 0, 0)),
            pl.BlockSpec((None, tm, fw), lambda j, i: (i, 0, 0)),
            pl.BlockSpec((None, d, tn), lambda j, i: (l, 0, j)),
            pl.BlockSpec((None, d, tn), lambda j, i: (l, 0, nd + j)),
            pl.BlockSpec((None, d, tn), lambda j, i: (l, 0, 2 * nd + j)),
            pl.BlockSpec((None, 1, tn), lambda j, i: (l, 0, j)),
            pl.BlockSpec((None, 1, tn), lambda j, i: (l, 0, nd + j)),
            pl.BlockSpec((None, 1, tn), lambda j, i: (l, 0, 2 * nd + j)),
            pl.BlockSpec((None, pool_w, tn), lambda j, i: (l, 0, j)),
            pl.BlockSpec((None, lru_w, tn), lambda j, i: (l, 0, j)),
            pl.BlockSpec((None, fw - pool_w - lru_w, tn), lambda j, i: (l, 0, j)),
        ],
        out_specs=pl.BlockSpec((None, tm, tn), lambda j, i: (i, 0, j)),
        out_shape=jax.ShapeDtypeStruct((nt, tm, d), BF16),
        scratch_shapes=[pltpu.VMEM((3, d, tn), BF16), pltpu.VMEM((fw, tn), BF16)],
        compiler_params=_params(vmem, 2),
        name="gate_merge",
    )(xn, f, wgate, wgate, wgate, bgate, bgate, bgate, wp, wl, wc)


def _outproj_kernel(m_ref, w_ref, *refs):
    *x_refs, o_ref, w_bf = refs

    @pl.when(pl.program_id(1) == 0)
    def _():
        w_bf[...] = w_ref[...].astype(BF16)

    acc = jnp.dot(m_ref[...], w_bf[...], preferred_element_type=F32)
    row = 0
    for x_ref in x_refs:
        rows = x_ref.shape[0]
        o_ref[row:row + rows, :] = x_ref[...] + acc[row:row + rows, :]
        row += rows


def _outproj(m, w, x, l, *, tn):
    nt, tm, k = m.shape
    d = w.shape[2]
    if isinstance(x, tuple):
        xs_in = x
        x_specs = [pl.BlockSpec((a.shape[0] // nt, tn), lambda j, i: (i, j)) for a in x]
    else:
        xs_in = (x,)
        x_specs = [pl.BlockSpec((None, tm, tn), lambda j, i: (i, 0, j))]
    vmem = 2 * tm * k * 2 + 2 * k * tn * 4 + k * tn * 2 + 4 * tm * tn * 4 + tm * tn * 4
    return pl.pallas_call(
        _outproj_kernel,
        grid=(d // tn, nt),
        in_specs=[
            pl.BlockSpec((None, tm, k), lambda j, i: (i, 0, 0)),
            pl.BlockSpec((None, k, tn), lambda j, i: (l, 0, j)),
        ] + x_specs,
        out_specs=pl.BlockSpec((None, tm, tn), lambda j, i: (i, 0, j)),
        out_shape=jax.ShapeDtypeStruct((nt, tm, d), F32),
        scratch_shapes=[pltpu.VMEM((k, tn), BF16)],
        compiler_params=_params(vmem, 2),
        name="outproj",
    )(m, w, *xs_in)


def _mlp_kernel(x_ref, g_ref, wup_ref, wdn_ref, o_ref, xn_ref):
    @pl.when(pl.program_id(1) == 0)
    def _():
        x = x_ref[...]
        xn_ref[...] = _rms_norm(x, g_ref[...]).astype(BF16)
        o_ref[...] = x

    hid = jnp.dot(xn_ref[...], wup_ref[...].astype(BF16), preferred_element_type=F32)
    hid = jnp.maximum(hid, 0.0)
    hid = (hid * hid).astype(BF16)
    o_ref[...] += jnp.dot(hid, wdn_ref[...].astype(BF16), preferred_element_type=F32)


def _mlp(x, g, wup, wdn, l, *, tf):
    nt, tm, d = x.shape
    dff = wup.shape[2]
    vmem = (tm * d * 4 + 2 * tm * d * 4 + tm * d * 2 + 4 * d * tf * 4 + 2 * d * tf * 2
            + 2 * tm * tf * 4)
    return pl.pallas_call(
        _mlp_kernel,
        grid=(nt, dff // tf),
        in_specs=[
            pl.BlockSpec((None, tm, d), lambda i, j: (i, 0, 0), pipeline_mode=pl.Buffered(1)),
            _layer_spec(g, l),
            pl.BlockSpec((None, d, tf), lambda i, j: (l, 0, j)),
            pl.BlockSpec((None, tf, d), lambda i, j: (l, j, 0)),
        ],
        out_specs=pl.BlockSpec((None, tm, d), lambda i, j: (i, 0, 0)),
        out_shape=jax.ShapeDtypeStruct((nt, tm, d), F32),
        scratch_shapes=[pltpu.VMEM((tm, d), BF16)],
        compiler_params=_params(vmem, 2),
        name="mlp",
    )(x, g, wup, wdn)


def _final_norm_kernel(x_ref, g_ref, yp_ref, ys_ref):
    rp = yp_ref.shape[0]
    y = _rms_norm(x_ref[...], g_ref[...])
    yp_ref[...] = y[0:rp, :]
    ys_ref[...] = y[rp:, :]


def _final_norm(x, g, *, rows_p):
    nt, tm, d = x.shape
    rs = tm - rows_p
    return pl.pallas_call(
        _final_norm_kernel,
        grid=(nt,),
        in_specs=[pl.BlockSpec((None, tm, d), lambda i: (i, 0, 0)),
                  pl.BlockSpec((1, d), lambda i: (0, 0))],
        out_specs=[pl.BlockSpec((rows_p, d), lambda i: (i, 0)),
                   pl.BlockSpec((rs, d), lambda i: (i, 0))],
        out_shape=[jax.ShapeDtypeStruct((nt * rows_p, d), F32),
                   jax.ShapeDtypeStruct((nt * rs, d), F32)],
        compiler_params=_params(5 * tm * d * 4, 1),
        name="final_norm",
    )(x, g)


def kernel(x_prompt, x_sample, state_pool, state_lru_conv, state_lru_h, state_conf_conv, g_mix, w_in, w_pool_grp, pool_scale, w_pool_br, w_lru_conv, b_lru_conv, w_lru_a, b_lru_a, w_lru_x, b_lru_x, lru_lambda, w_lru_br, w_conf_conv, b_conf_conv, g_conf, b_conf, w_conf_br, w_gate, b_gate, w_out, g_mlp, w_up, w_down, g_final):
    batch, seq, d = x_prompt.shape
    nb, dec_seq, _ = x_sample.shape
    assert dec_seq == 1
    depth = w_in.shape[0]
    pool_w = w_pool_br.shape[1]
    lru_w = w_lru_br.shape[1]
    conf_w = w_conf_br.shape[1]
    pool_buf = state_pool.shape[2]
    lru_buf = state_lru_conv.shape[2]
    conf_buf = state_conf_conv.shape[2]
    n_prompt = batch * seq
    rows_p = n_prompt // NT
    widths = dict(pool_w=pool_w, lru_w=lru_w, conf_w=conf_w)

    row = lambda a: a[:, None, :]
    g_mix_r, g_mlp_r, b_gate_r = row(g_mix), row(g_mlp), row(b_gate)
    pscale_r, blc_r, ba_r, bx_r, lam_r = row(pool_scale), row(b_lru_conv), row(b_lru_a), row(b_lru_x), row(lru_lambda)
    bcc_r, gcf_r, bcf_r = row(b_conf_conv), row(g_conf), row(b_conf)

    xp, xs = x_prompt.reshape(n_prompt, d), x_sample.reshape(nb, d)
    tm = rows_p + nb // NT
    x = None
    outs = [[] for _ in range(4)]
    new_states = ()
    mix_w = (w_pool_grp, pscale_r, w_lru_conv, blc_r, w_lru_a, w_lru_x,
             ba_r, bx_r, lam_r, w_conf_conv, bcc_r, gcf_r, bcf_r)
    w_in_bf = _cast_bf16(w_in, rows=256)
    for l in range(depth):
        xn, f, p_last, l_last, h_last, v_last = _inmix(
            xp if l == 0 else x, g_mix_r, w_in_bf, *mix_w, l,
            batch=batch, seq=seq, rows_p=rows_p, tm=tm, tt=256, **widths)
        xn, zs = _inproj_sample(xn, xs if l == 0 else x, g_mix_r, w_in_bf, l, rows_p=rows_p)
        f, *new_states = _mixer_sample(
            f, zs, state_pool, state_lru_conv, state_lru_h, state_conf_conv, *mix_w, l,
            tuple(new_states),
            rows_p=rows_p, start_pos=PAST_LEN, **widths)
        merged = _gate_merge(xn, f, w_gate, b_gate_r, w_pool_br, w_lru_br, w_conf_br, l, tn=256)
        x = _outproj(merged, w_out, (xp, xs) if l == 0 else x, l, tn=1024)
        x = _mlp(x, g_mlp_r, w_up, w_down, l, tf=512)

        outs[0].append(p_last[:, POOL_HALO - pool_buf:])
        outs[1].append(l_last[:, LRU_HALO - lru_buf:])
        outs[2].append(h_last[:, SUBLANE - 1])
        outs[3].append(v_last[:, CONF_HALO - conf_buf:])

    y_p, y_s = _final_norm(x, g_final[None, :], rows_p=rows_p)
    return ((y_p.reshape(batch, seq, d), y_s.reshape(nb, 1, d))
            + tuple(jnp.stack(o) for o in outs) + tuple(new_states))
```

```python
import functools
import math

import jax
import jax.numpy as jnp
from jax import lax
from jax.experimental import pallas as pl
from jax.experimental.pallas import tpu as pltpu

F32 = jnp.float32
BF16 = jnp.bfloat16

POOL_WINDOWS = (2, 4, 8, 16)
LRU_C = 8.0
EPS = 1e-6
PAST_LEN = 16384
LANE = 128
SUBLANE = 8
NT = 8
VMEM_SLACK = 8 << 20
VMEM_CAP = 60 << 20


def _sigmoid(x):
    return 0.5 * jnp.tanh(0.5 * x) + 0.5


def _gelu_tanh(x):
    c = math.sqrt(2.0 / math.pi)
    return x * (0.5 * (1.0 + jnp.tanh(c * (x + 0.044715 * (x * x * x)))))


def _softplus(x):
    return jnp.maximum(x, 0.0) + jnp.log1p(jnp.exp(-jnp.abs(x)))


def _rms_norm(x, g):
    ms = jnp.mean(x * x, axis=-1, keepdims=True)
    return (x * lax.rsqrt(ms + EPS)) * g


def _layer_spec(a, l):
    zeros = (0,) * (a.ndim - 1)
    return pl.BlockSpec((None,) + a.shape[1:], lambda *_: (l,) + zeros)


def _params(vmem_bytes, n_axes):
    return pltpu.CompilerParams(
        dimension_semantics=("arbitrary",) * n_axes,
        vmem_limit_bytes=int(min(vmem_bytes + VMEM_SLACK, VMEM_CAP)))


def _cast_kernel(w_ref, o_ref):
    o_ref[...] = w_ref[...].astype(BF16)


def _cast_bf16(w, *, rows):
    depth, k, n = w.shape
    return pl.pallas_call(
        _cast_kernel,
        grid=(depth, k // rows),
        in_specs=[pl.BlockSpec((None, rows, n), lambda l, i: (l, i, 0))],
        out_specs=pl.BlockSpec((None, rows, n), lambda l, i: (l, i, 0)),
        out_shape=jax.ShapeDtypeStruct(w.shape, BF16),
        compiler_params=_params(2 * rows * n * 6, 2),
        name="cast_w_in",
    )(w)


def _lru_coeffs(ri, ba, bx, sp):
    gate_r = _sigmoid(ri[:, :LANE] + ba)
    gate_i = _sigmoid(ri[:, LANE:] + bx)
    log_a = (-LRU_C) * gate_r * sp
    a = jnp.exp(log_a)
    th = jnp.tanh(log_a)
    mult = jnp.sqrt((-2.0 * th) / (1.0 - th))
    return a, gate_i, mult


def _layer_norm_silu(vc, g, b):
    mu = jnp.mean(vc, axis=-1, keepdims=True)
    cen = vc - mu
    var = jnp.mean(cen * cen, axis=-1, keepdims=True)
    y = (cen * lax.rsqrt(var + EPS)) * g + b
    return y * _sigmoid(y)


def _scan_tile(a, b, carry, tt):
    groups = tt // SUBLANE
    a3 = a.reshape(groups, SUBLANE, LANE)
    b3 = b.reshape(groups, SUBLANE, LANE)
    sub = lax.broadcasted_iota(jnp.int32, (groups, SUBLANE, LANE), 1)
    for k in (1, 2, 4):
        keep = sub >= k
        b_sh = jnp.where(keep, pltpu.roll(b3, k, 1), 0.0)
        a_sh = jnp.where(keep, pltpu.roll(a3, k, 1), 1.0)
        b3 = b3 + a3 * b_sh
        a3 = a3 * a_sh
    hs = []
    for j in range(groups):
        hj = b3[j] + a3[j] * carry
        hs.append(hj)
        carry = hj[SUBLANE - 1:SUBLANE, :]
    return jnp.concatenate(hs, axis=0)


POOL_HALO = 16
LRU_HALO = 8
CONF_HALO = 32
LN_ROWS = 64


def _mix_reset(pool_ext, lru_ext, conf_ext, h_carry):
    pool_ext[0:POOL_HALO, :] = jnp.zeros((POOL_HALO, pool_ext.shape[1]), F32)
    lru_ext[0:LRU_HALO, :] = jnp.zeros((LRU_HALO, lru_ext.shape[1]), F32)
    conf_ext[0:CONF_HALO, :] = jnp.zeros((CONF_HALO, conf_ext.shape[1]), F32)
    h_carry[...] = jnp.zeros_like(h_carry)


def _mix_tile(t, z_ref, wgrp_ref, pscale_ref, wlc_ref, blc_ref, wa_ref, wx_ref, ba_ref,
              bx_ref, lam_ref, wcc_ref, bcc_ref, gcf_ref, bcf_ref,
              f_ref, plast_ref, llast_ref, hlast_ref, vlast_ref,
              pool_ext, lru_ext, conf_ext, vc_buf, h_carry,
              *, tt, pool_w, lru_w, conf_w, lru_taps, conf_taps, tick):
    c_lru = pool_w
    c_gel = c_lru + lru_w
    c_ca = c_gel + lru_w
    c_cb = c_ca + conf_w

    row = lax.broadcasted_iota(jnp.int32, (tt, LANE), 0)
    pos = row + t * tt
    posf = pos.astype(F32)
    is_first = pos == 0
    never = posf[0:SUBLANE, :] < 0.0

    tick()
    pool_ext[POOL_HALO:POOL_HALO + tt, :] = z_ref[:, 0:pool_w]
    for g, w in enumerate(POOL_WINDOWS):
        lanes = slice(g * LANE, (g + 1) * LANE)
        u = pool_ext[POOL_HALO:POOL_HALO + tt, lanes]
        s = u
        for j in range(1, w):
            s = s + pool_ext[POOL_HALO - j:POOL_HALO - j + tt, lanes]
        cnt = jnp.minimum(jnp.float32(w), posf + 1.0)
        d = s / cnt - u
        y = jnp.dot(d.astype(BF16), wgrp_ref[g].astype(BF16), preferred_element_type=F32)
        f_ref[:, lanes] = (y * pscale_ref[:, lanes]).astype(BF16)
    plast_ref[...] = pool_ext[tt:tt + POOL_HALO, :]
    pool_ext[0:POOL_HALO, :] = pool_ext[tt:tt + POOL_HALO, :]

    lru_ext[LRU_HALO:LRU_HALO + tt, :] = z_ref[:, c_lru:c_lru + lru_w]
    sp_all = _softplus(-lam_ref[...])
    for n in range(lru_w // LANE):
        lanes = slice(n * LANE, (n + 1) * LANE)
        xc = blc_ref[:, lanes] + wlc_ref[lru_taps - 1:lru_taps, lanes] * lru_ext[LRU_HALO:LRU_HALO + tt, lanes]
        for k in range(lru_taps - 1):
            off = LRU_HALO - (lru_taps - 1) + k
            xc = xc + wlc_ref[k:k + 1, lanes] * lru_ext[off:off + tt, lanes]
        xcb = xc.astype(BF16)
        ri = jnp.concatenate(
            [jnp.dot(xcb, wa_ref[n].astype(BF16), preferred_element_type=F32),
             jnp.dot(xcb, wx_ref[n].astype(BF16), preferred_element_type=F32)], axis=1)
        tick()
        a, gate_i, mult = _lru_coeffs(ri, ba_ref[:, lanes], bx_ref[:, lanes], sp_all[:, lanes])
        a = jnp.where(is_first, 0.0, a)
        mult = jnp.where(is_first, 1.0, mult)
        b = xc * gate_i * mult
        h = _scan_tile(a, b, h_carry[SUBLANE - 1:SUBLANE, lanes], tt)
        h_carry[:, lanes] = h[tt - SUBLANE:tt, :]
        hlast_ref[:, lanes] = h[tt - SUBLANE:tt, :]
        gel = _gelu_tanh(z_ref[:, c_gel + n * LANE:c_gel + (n + 1) * LANE])
        f_ref[:, pool_w + n * LANE:pool_w + (n + 1) * LANE] = (h * gel).astype(BF16)
    llast_ref[...] = lru_ext[tt:tt + LRU_HALO, :]
    lru_ext[0:LRU_HALO, :] = lru_ext[tt:tt + LRU_HALO, :]

    conf_ext[CONF_HALO:CONF_HALO + tt, :] = (
        z_ref[:, c_ca:c_ca + conf_w] * _sigmoid(z_ref[:, c_cb:c_cb + conf_w]))
    base = CONF_HALO - (conf_taps - 1)
    for c in range(conf_w // LANE):
        lanes = slice(c * LANE, (c + 1) * LANE)
        tick()
        acc = None
        for r in range(SUBLANE):
            offs = [o for o in range(base, CONF_HALO + 1) if o % SUBLANE == r]
            if not offs:
                continue
            part = None
            for o in offs:
                k = o - base
                term = wcc_ref[k:k + 1, lanes] * conf_ext[o:o + tt, lanes]
                part = term if part is None else part + term
            acc = part if acc is None else acc + part
        vc = acc + bcc_ref[:, lanes]
        pace = jnp.dot(vc[0:SUBLANE, :].astype(BF16), wgrp_ref[0].astype(BF16),
                       preferred_element_type=F32)
        vc_buf[0:SUBLANE, lanes] = jnp.where(never, pace, vc[0:SUBLANE, :])
        vc_buf[SUBLANE:, lanes] = vc[SUBLANE:, :]
    tick()
    for r in range(tt // LN_ROWS):
        rows = slice(r * LN_ROWS, (r + 1) * LN_ROWS)
        y = _layer_norm_silu(vc_buf[rows, :], gcf_ref[...], bcf_ref[...])
        f_ref[rows, pool_w + lru_w:pool_w + lru_w + conf_w] = y.astype(BF16)
    vlast_ref[...] = conf_ext[tt:tt + CONF_HALO, :]
    conf_ext[0:CONF_HALO, :] = conf_ext[tt:tt + CONF_HALO, :]


def _inmix_kernel(x_ref, g_ref, w_ref, *refs, n_tiles, steps_per_seq, mix):
    n_w = 13
    weights = refs[:n_w]
    xn_ref, f_ref, plast_ref, llast_ref, hlast_ref, vlast_ref = refs[n_w:n_w + 6]
    z_a, z_b, pool_ext, lru_ext, conf_ext, vc_buf, h_carry = refs[n_w + 6:]
    s = pl.program_id(0)
    t_prev = lax.rem(s + (steps_per_seq - 1), steps_per_seq)

    @pl.when(s == 0)
    def _():
        z_b[...] = jnp.zeros_like(z_b)
        _mix_reset(pool_ext, lru_ext, conf_ext, h_carry)

    @pl.when(jnp.logical_and(s > 0, t_prev == 0))
    def _():
        _mix_reset(pool_ext, lru_ext, conf_ext, h_carry)

    zc = w_ref.shape[1]
    sections = 1 + mix["lru_w"] // LANE + mix["conf_w"] // LANE + 1
    cw = zc // sections

    def step(z_w, z_r):
        xn_ref[...] = _rms_norm(x_ref[...], g_ref[...]).astype(BF16)
        done = [0]

        def tick():
            c0 = done[0] * cw
            z_w[:, c0:c0 + cw] = jnp.dot(xn_ref[...], w_ref[:, c0:c0 + cw],
                                         preferred_element_type=F32)
            done[0] += 1

        _mix_tile(t_prev, z_r, *weights, f_ref, plast_ref, llast_ref, hlast_ref, vlast_ref,
                  pool_ext, lru_ext, conf_ext, vc_buf, h_carry, tick=tick, **mix)
        assert done[0] * cw == zc

    @pl.when(lax.rem(s, 2) == 0)
    def _():
        step(z_a, z_b)

    @pl.when(lax.rem(s, 2) == 1)
    def _():
        step(z_b, z_a)


def _inmix(x, g, w_bf, wgrp, pscale, wlc, blc, wa, wx, ba, bx, lam, wcc, bcc, gcf, bcf, l,
           *, batch, seq, rows_p, tm, tt, pool_w, lru_w, conf_w):
    d = x.shape[-1]
    nt = batch * seq // rows_p
    zc = w_bf.shape[2]
    steps_per_tile = rows_p // tt
    steps_per_seq = seq // tt
    n_tiles = batch * steps_per_seq
    fw = pool_w + lru_w + conf_w
    mix = dict(tt=tt, pool_w=pool_w, lru_w=lru_w, conf_w=conf_w,
               lru_taps=wlc.shape[1], conf_taps=wcc.shape[1])

    def cur(s):
        c = jnp.minimum(s, n_tiles - 1)
        return (c // steps_per_tile, c % steps_per_tile, 0)

    def prev(s):
        p = jnp.maximum(s - 1, 0)
        return (p // steps_per_tile, p % steps_per_tile, 0)

    smap = lambda s: (jnp.maximum(s - 1, 0) // steps_per_seq, 0, 0)
    vmem = (d * zc * 2 + 2 * tt * zc * 4 + 2 * tt * d * 4 + 2 * tt * d * 2 + 2 * tt * fw * 2
            + (POOL_HALO + tt) * pool_w * 4 + (LRU_HALO + tt) * lru_w * 4
            + (CONF_HALO + tt) * conf_w * 4 + tt * conf_w * 4 + (10 << 20))
    weights = (wgrp, pscale, wlc, blc, wa, wx, ba, bx, lam, wcc, bcc, gcf, bcf)
    x_spec = (pl.BlockSpec((tt, d), lambda s: (jnp.minimum(s, n_tiles - 1), 0)) if x.ndim == 2
              else pl.BlockSpec((None, tt, d), cur))
    return pl.pallas_call(
        functools.partial(_inmix_kernel, n_tiles=n_tiles, steps_per_seq=steps_per_seq, mix=mix),
        grid=(n_tiles + 1,),
        in_specs=[x_spec,
                  _layer_spec(g, l),
                  pl.BlockSpec((None, d, zc), lambda s: (l, 0, 0), pipeline_mode=pl.Buffered(1))]
                 + [_layer_spec(a, l) for a in weights],
        out_specs=[
            pl.BlockSpec((None, tt, d), cur),
            pl.BlockSpec((None, tt, fw), prev),
            pl.BlockSpec((None, POOL_HALO, pool_w), smap),
            pl.BlockSpec((None, LRU_HALO, lru_w), smap),
            pl.BlockSpec((None, SUBLANE, lru_w), smap),
            pl.BlockSpec((None, CONF_HALO, conf_w), smap),
        ],
        out_shape=[
            jax.ShapeDtypeStruct((nt, tm, d), BF16),
            jax.ShapeDtypeStruct((nt, tm, fw), BF16),
            jax.ShapeDtypeStruct((batch, POOL_HALO, pool_w), F32),
            jax.ShapeDtypeStruct((batch, LRU_HALO, lru_w), F32),
            jax.ShapeDtypeStruct((batch, SUBLANE, lru_w), F32),
            jax.ShapeDtypeStruct((batch, CONF_HALO, conf_w), F32),
        ],
        scratch_shapes=[
            pltpu.VMEM((tt, zc), F32),
            pltpu.VMEM((tt, zc), F32),
            pltpu.VMEM((POOL_HALO + tt, pool_w), F32),
            pltpu.VMEM((LRU_HALO + tt, lru_w), F32),
            pltpu.VMEM((CONF_HALO + tt, conf_w), F32),
            pltpu.VMEM((tt, conf_w), F32),
            pltpu.VMEM((SUBLANE, lru_w), F32),
        ],
        compiler_params=_params(vmem, 1),
        name="inmix",
    )(x, g, w_bf, *weights)


def _inproj_sample_kernel(xn_any, x_ref, g_ref, w_ref, xn_ref, zs_ref, xs_buf):
    del xn_any
    i = pl.program_id(0)
    rs = x_ref.shape[0]
    xn = _rms_norm(x_ref[...], g_ref[...]).astype(BF16)
    xn_ref[...] = xn
    xs_buf[pl.ds(pl.multiple_of(i * rs, rs), rs), :] = xn

    @pl.when(i == pl.num_programs(0) - 1)
    def _():
        zs_ref[...] = jnp.dot(xs_buf[...], w_ref[...], preferred_element_type=F32)


def _inproj_sample(xn, x, g, w_bf, l, *, rows_p):
    nt, tm, d = xn.shape
    rs = tm - rows_p
    zc = w_bf.shape[2]
    rowmap = lambda i: (i, rows_p // rs, 0)
    x_spec = (pl.BlockSpec((rs, d), lambda i: (i, 0)) if x.ndim == 2
              else pl.BlockSpec((None, rs, d), rowmap))
    vmem = d * zc * 2 + 2 * nt * rs * zc * 4 + nt * rs * d * 2 + (4 << 20)
    return pl.pallas_call(
        _inproj_sample_kernel,
        grid=(nt,),
        in_specs=[pl.BlockSpec(memory_space=pl.ANY),
                  x_spec,
                  _layer_spec(g, l),
                  pl.BlockSpec((None, d, zc), lambda i: (l, 0, 0), pipeline_mode=pl.Buffered(1))],
        out_specs=[pl.BlockSpec((None, rs, d), rowmap),
                   pl.BlockSpec((nt * rs, zc), lambda i: (0, 0))],
        out_shape=[jax.ShapeDtypeStruct(xn.shape, xn.dtype),
                   jax.ShapeDtypeStruct((nt * rs, zc), F32)],
        input_output_aliases={0: 0},
        scratch_shapes=[pltpu.VMEM((nt * rs, d), BF16)],
        compiler_params=_params(vmem, 1),
        name="inproj_sample",
    )(xn, x, g, w_bf)


def _mixer_sample_kernel(f_any, z_ref, spool_ref, slru_ref, sh_ref, sconf_ref,
                         wgrp_ref, pscale_ref, wlc_ref, blc_ref, wa_ref, wx_ref, ba_ref,
                         bx_ref, lam_ref, wcc_ref, bcc_ref, gcf_ref, bcf_ref, *rest,
                         pool_w, lru_w, conf_w, lru_taps, conf_taps, pool_buf, start_pos):
    f_ref, npool_ref, nlru_ref, hnew_ref, nconf_ref, vc_buf = rest[-6:]
    del f_any
    c_lru = pool_w
    c_gel = c_lru + lru_w
    c_ca = c_gel + lru_w
    c_cb = c_ca + conf_w

    for g, w in enumerate(POOL_WINDOWS):
        lanes = slice(g * LANE, (g + 1) * LANE)
        u = z_ref[:, lanes]
        s = u
        for j in range(1, w):
            trow = pool_buf - j
            s = s + spool_ref[:, trow, lanes]
        cnt = float(min(w, start_pos + 1))
        d = s / cnt - u
        y = jnp.dot(d.astype(BF16), wgrp_ref[g].astype(BF16), preferred_element_type=F32)
        f_ref[:, lanes] = (y * pscale_ref[:, lanes]).astype(BF16)
    npool_ref[:, 0:pool_buf - 1, :] = spool_ref[:, 1:pool_buf, :]
    npool_ref[:, pool_buf - 1, :] = z_ref[:, 0:pool_w]

    sp_all = _softplus(-lam_ref[...])
    for n in range(lru_w // LANE):
        lanes = slice(n * LANE, (n + 1) * LANE)
        xc = blc_ref[:, lanes] + wlc_ref[lru_taps - 1:lru_taps, lanes] * z_ref[:, c_lru + n * LANE:c_lru + (n + 1) * LANE]
        for k in range(lru_taps - 1):
            xc = xc + wlc_ref[k:k + 1, lanes] * slru_ref[:, k, lanes]
        xcb = xc.astype(BF16)
        ri = jnp.concatenate(
            [jnp.dot(xcb, wa_ref[n].astype(BF16), preferred_element_type=F32),
             jnp.dot(xcb, wx_ref[n].astype(BF16), preferred_element_type=F32)], axis=1)
        a, gate_i, mult = _lru_coeffs(ri, ba_ref[:, lanes], bx_ref[:, lanes], sp_all[:, lanes])
        if start_pos == 0:
            a = jnp.zeros_like(a)
            mult = jnp.ones_like(mult)
        h = a * sh_ref[:, lanes] + xc * gate_i * mult
        hnew_ref[:, lanes] = h
        gel = _gelu_tanh(z_ref[:, c_gel + n * LANE:c_gel + (n + 1) * LANE])
        f_ref[:, pool_w + n * LANE:pool_w + (n + 1) * LANE] = (h * gel).astype(BF16)
    nlru_ref[:, 0:lru_taps - 2, :] = slru_ref[:, 1:lru_taps - 1, :]
    nlru_ref[:, lru_taps - 2, :] = z_ref[:, c_lru:c_lru + lru_w]

    for c in range(conf_w // LANE):
        lanes = slice(c * LANE, (c + 1) * LANE)
        v = z_ref[:, c_ca + c * LANE:c_ca + (c + 1) * LANE] * _sigmoid(
            z_ref[:, c_cb + c * LANE:c_cb + (c + 1) * LANE])
        nconf_ref[:, conf_taps - 2, lanes] = v
        acc = bcc_ref[:, lanes] + wcc_ref[conf_taps - 1:conf_taps, lanes] * v
        for k in range(conf_taps - 1):
            acc = acc + wcc_ref[k:k + 1, lanes] * sconf_ref[:, k, lanes]
        vc_buf[:, lanes] = acc
    nconf_ref[:, 0:conf_taps - 2, :] = sconf_ref[:, 1:conf_taps - 1, :]
    y = _layer_norm_silu(vc_buf[...], gcf_ref[...], bcf_ref[...])
    f_ref[:, pool_w + lru_w:pool_w + lru_w + conf_w] = y.astype(BF16)


def _mixer_sample(f, z, spool, slru, sh, sconf, wgrp, pscale, wlc, blc, wa, wx, ba, bx, lam,
                  wcc, bcc, gcf, bcf, l, new_states, *, rows_p, pool_w, lru_w, conf_w, start_pos):
    nt, tm, _ = f.shape
    zc = z.shape[1]
    rs = tm - rows_p
    fw = pool_w + lru_w + conf_w
    lru_taps = wlc.shape[1]
    conf_taps = wcc.shape[1]
    pool_buf = spool.shape[2]
    rowmap = lambda i: (i, rows_p // rs, 0)
    bmap = lambda i: (i, 0)
    states = (spool, slru, sh, sconf)
    weights = (wgrp, pscale, wlc, blc, wa, wx, ba, bx, lam, wcc, bcc, gcf, bcf)
    n_in = 2 + len(states) + len(weights)

    def state_spec(s):
        zeros = (0,) * (s.ndim - 2)
        return pl.BlockSpec((None, rs) + s.shape[2:], lambda i: (l, i) + zeros)

    vmem = (4 * rs * sum(math.prod(s.shape[2:]) for s in states) * 4 * 2
            + 2 * sum(a[0].size * a.dtype.itemsize for a in weights) + (8 << 20))
    kern = functools.partial(_mixer_sample_kernel, pool_w=pool_w, lru_w=lru_w, conf_w=conf_w,
                             lru_taps=lru_taps, conf_taps=conf_taps, pool_buf=pool_buf,
                             start_pos=start_pos)
    return pl.pallas_call(
        kern,
        grid=(nt,),
        in_specs=[pl.BlockSpec(memory_space=pl.ANY),
                  pl.BlockSpec((rs, zc), bmap)]
                 + [state_spec(s) for s in states]
                 + [_layer_spec(a, l) for a in weights]
                 + [pl.BlockSpec(memory_space=pl.ANY) for _ in new_states],
        out_specs=[pl.BlockSpec((None, rs, fw), rowmap)] + [state_spec(s) for s in states],
        out_shape=[jax.ShapeDtypeStruct(f.shape, f.dtype)]
                  + [jax.ShapeDtypeStruct(s.shape, F32) for s in states],
        input_output_aliases={0: 0, **{n_in + k: 1 + k for k in range(len(new_states))}},
        scratch_shapes=[pltpu.VMEM((rs, conf_w), F32)],
        compiler_params=_params(vmem, 1),
        name="mixer_sample",
    )(f, z, *states, *weights, *new_states)


def _gate_merge_kernel(xn_ref, f_ref, wga_ref, wgb_ref, wgc_ref, bga_ref, bgb_ref, bgc_ref,
                       wp_ref, wl_ref, wc_ref, o_ref, wg_bf, wb_bf, *, pool_w, lru_w):
    c1 = pool_w
    c2 = pool_w + lru_w

    @pl.when(pl.program_id(1) == 0)
    def _():
        wg_bf[0] = wga_ref[...].astype(BF16)
        wg_bf[1] = wgb_ref[...].astype(BF16)
        wg_bf[2] = wgc_ref[...].astype(BF16)
        wb_bf[0:c1, :] = wp_ref[...].astype(BF16)
        wb_bf[c1:c2, :] = wl_ref[...].astype(BF16)
        wb_bf[c2:, :] = wc_ref[...].astype(BF16)

    xn = xn_ref[...]
    ga = _sigmoid(jnp.dot(xn, wg_bf[0], preferred_element_type=F32) + bga_ref[...])
    m = ga * jnp.dot(f_ref[:, 0:c1], wb_bf[0:c1, :], preferred_element_type=F32)
    gb = _sigmoid(jnp.dot(xn, wg_bf[1], preferred_element_type=F32) + bgb_ref[...])
    m = m + gb * jnp.dot(f_ref[:, c1:c2], wb_bf[c1:c2, :], preferred_element_type=F32)
    gc = _sigmoid(jnp.dot(xn, wg_bf[2], preferred_element_type=F32) + bgc_ref[...])
    m = m + gc * jnp.dot(f_ref[:, c2:], wb_bf[c2:, :], preferred_element_type=F32)
    o_ref[...] = m.astype(BF16)


def _gate_merge(xn, f, wgate, bgate, wp, wl, wc, l, *, tn):
    nt, tm, d = xn.shape
    fw = f.shape[2]
    pool_w, lru_w = wp.shape[1], wl.shape[1]
    nd = d // tn
    vmem = (2 * tm * (d + fw) * 2 + 2 * (3 * d + fw) * tn * 4 + (3 * d + fw) * tn * 2
            + 2 * tm * tn * 2 + 6 * tm * tn * 4)
    return pl.pallas_call(
        functools.partial(_gate_merge_kernel, pool_w=pool_w, lru_w=lru_w),
        grid=(nd, nt),
        in_specs=[
            pl.BlockSpec((None, tm, d), lambda j, i: (i, 0, 0)),
            pl.BlockSpec((None, tm, fw), lambda j, i: (i, 0, 0)),
            pl.BlockSpec((None, d, tn), lambda j, i: (l, 0, j)),
            pl.BlockSpec((None, d, tn), lambda j, i: (l, 0, nd + j)),
            pl.BlockSpec((None, d, tn), lambda j, i: (l, 0, 2 * nd + j)),
            pl.BlockSpec((None, 1, tn), lambda j, i: (l, 0, j)),
            pl.BlockSpec((None, 1, tn), lambda j, i: (l, 0, nd + j)),
            pl.BlockSpec((None, 1, tn), lambda j, i: (l, 0, 2 * nd + j)),
            pl.BlockSpec((None, pool_w, tn), lambda j, i: (l, 0, j)),
            pl.BlockSpec((None, lru_w, tn), lambda j, i: (l, 0, j)),
            pl.BlockSpec((None, fw - pool_w - lru_w, tn), lambda j, i: (l, 0, j)),
        ],
        out_specs=pl.BlockSpec((None, tm, tn), lambda j, i: (i, 0, j)),
        out_shape=jax.ShapeDtypeStruct((nt, tm, d), BF16),
        scratch_shapes=[pltpu.VMEM((3, d, tn), BF16), pltpu.VMEM((fw, tn), BF16)],
        compiler_params=_params(vmem, 2),
        name="gate_merge",
    )(xn, f, wgate, wgate, wgate, bgate, bgate, bgate, wp, wl, wc)


def _outproj_kernel(m_ref, w_ref, *refs):
    *x_refs, o_ref, w_bf = refs

    @pl.when(pl.program_id(1) == 0)
    def _():
        w_bf[...] = w_ref[...].astype(BF16)

    acc = jnp.dot(m_ref[...], w_bf[...], preferred_element_type=F32)
    row = 0
    for x_ref in x_refs:
        rows = x_ref.shape[0]
        o_ref[row:row + rows, :] = x_ref[...] + acc[row:row + rows, :]
        row += rows


def _outproj(m, w, x, l, *, tn):
    nt, tm, k = m.shape
    d = w.shape[2]
    if isinstance(x, tuple):
        xs_in = x
        x_specs = [pl.BlockSpec((a.shape[0] // nt, tn), lambda j, i: (i, j)) for a in x]
    else:
        xs_in = (x,)
        x_specs = [pl.BlockSpec((None, tm, tn), lambda j, i: (i, 0, j))]
    vmem = 2 * tm * k * 2 + 2 * k * tn * 4 + k * tn * 2 + 4 * tm * tn * 4 + tm * tn * 4
    return pl.pallas_call(
        _outproj_kernel,
        grid=(d // tn, nt),
        in_specs=[
            pl.BlockSpec((None, tm, k), lambda j, i: (i, 0, 0)),
            pl.BlockSpec((None, k, tn), lambda j, i: (l, 0, j)),
        ] + x_specs,
        out_specs=pl.BlockSpec((None, tm, tn), lambda j, i: (i, 0, j)),
        out_shape=jax.ShapeDtypeStruct((nt, tm, d), F32),
        scratch_shapes=[pltpu.VMEM((k, tn), BF16)],
        compiler_params=_params(vmem, 2),
        name="outproj",
    )(m, w, *xs_in)


def _mlp_kernel(x_ref, g_ref, wup_ref, wdn_ref, o_ref, xn_ref):
    @pl.when(pl.program_id(1) == 0)
    def _():
        x = x_ref[...]
        xn_ref[...] = _rms_norm(x, g_ref[...]).astype(BF16)
        o_ref[...] = x

    hid = jnp.dot(xn_ref[...], wup_ref[...].astype(BF16), preferred_element_type=F32)
    hid = jnp.maximum(hid, 0.0)
    hid = (hid * hid).astype(BF16)
    o_ref[...] += jnp.dot(hid, wdn_ref[...].astype(BF16), preferred_element_type=F32)


def _mlp(x, g, wup, wdn, l, *, tf):
    nt, tm, d = x.shape
    dff = wup.shape[2]
    vmem = (tm * d * 4 + 2 * tm * d * 4 + tm * d * 2 + 4 * d * tf * 4 + 2 * d * tf * 2
            + 2 * tm * tf * 4)
    return pl.pallas_call(
        _mlp_kernel,
        grid=(nt, dff // tf),
        in_specs=[
            pl.BlockSpec((None, tm, d), lambda i, j: (i, 0, 0), pipeline_mode=pl.Buffered(1)),
            _layer_spec(g, l),
            pl.BlockSpec((None, d, tf), lambda i, j: (l, 0, j)),
            pl.BlockSpec((None, tf, d), lambda i, j: (l, j, 0)),
        ],
        out_specs=pl.BlockSpec((None, tm, d), lambda i, j: (i, 0, 0)),
        out_shape=jax.ShapeDtypeStruct((nt, tm, d), F32),
        scratch_shapes=[pltpu.VMEM((tm, d), BF16)],
        compiler_params=_params(vmem, 2),
        name="mlp",
    )(x, g, wup, wdn)


def _mlp_final_kernel(x_ref, g_ref, wup_ref, wdn_ref, gf_ref, yp_ref, ys_ref, xn_ref):
    rp = yp_ref.shape[0]

    @pl.when(pl.program_id(1) == 0)
    def _():
        x = x_ref[...]
        xn_ref[...] = _rms_norm(x, g_ref[...]).astype(BF16)
        yp_ref[...] = x[0:rp, :]
        ys_ref[...] = x[rp:, :]

    hid = jnp.dot(xn_ref[...], wup_ref[...].astype(BF16), preferred_element_type=F32)
    hid = jnp.maximum(hid, 0.0)
    hid = (hid * hid).astype(BF16)
    contrib = jnp.dot(hid, wdn_ref[...].astype(BF16), preferred_element_type=F32)
    yp_ref[...] += contrib[0:rp, :]
    ys_ref[...] += contrib[rp:, :]

    @pl.when(pl.program_id(1) == pl.num_programs(1) - 1)
    def _():
        yp_ref[...] = _rms_norm(yp_ref[...], gf_ref[...])
        ys_ref[...] = _rms_norm(ys_ref[...], gf_ref[...])


def _mlp_final(x, g, wup, wdn, g_final, l, *, tf, rows_p):
    nt, tm, d = x.shape
    rs = tm - rows_p
    dff = wup.shape[2]
    vmem = (tm * d * 4 + 2 * tm * d * 4 + tm * d * 2 + 4 * d * tf * 4 + 2 * d * tf * 2
            + 2 * tm * tf * 4)
    return pl.pallas_call(
        _mlp_final_kernel,
        grid=(nt, dff // tf),
        in_specs=[
            pl.BlockSpec((None, tm, d), lambda i, j: (i, 0, 0), pipeline_mode=pl.Buffered(1)),
            _layer_spec(g, l),
            pl.BlockSpec((None, d, tf), lambda i, j: (l, 0, j)),
            pl.BlockSpec((None, tf, d), lambda i, j: (l, j, 0)),
            pl.BlockSpec((1, d), lambda i, j: (0, 0)),
        ],
        out_specs=[pl.BlockSpec((rows_p, d), lambda i, j: (i, 0)),
                   pl.BlockSpec((rs, d), lambda i, j: (i, 0))],
        out_shape=[jax.ShapeDtypeStruct((nt * rows_p, d), F32),
                   jax.ShapeDtypeStruct((nt * rs, d), F32)],
        scratch_shapes=[pltpu.VMEM((tm, d), BF16)],
        compiler_params=_params(vmem, 2),
        name="mlp_final",
    )(x, g, wup, wdn, g_final)


def kernel(x_prompt, x_sample, state_pool, state_lru_conv, state_lru_h, state_conf_conv, g_mix, w_in, w_pool_grp, pool_scale, w_pool_br, w_lru_conv, b_lru_conv, w_lru_a, b_lru_a, w_lru_x, b_lru_x, lru_lambda, w_lru_br, w_conf_conv, b_conf_conv, g_conf, b_conf, w_conf_br, w_gate, b_gate, w_out, g_mlp, w_up, w_down, g_final):
    batch, seq, d = x_prompt.shape
    nb, dec_seq, _ = x_sample.shape
    assert dec_seq == 1
    depth = w_in.shape[0]
    pool_w = w_pool_br.shape[1]
    lru_w = w_lru_br.shape[1]
    conf_w = w_conf_br.shape[1]
    pool_buf = state_pool.shape[2]
    lru_buf = state_lru_conv.shape[2]
    conf_buf = state_conf_conv.shape[2]
    n_prompt = batch * seq
    rows_p = n_prompt // NT
    widths = dict(pool_w=pool_w, lru_w=lru_w, conf_w=conf_w)

    row = lambda a: a[:, None, :]
    g_mix_r, g_mlp_r, b_gate_r = row(g_mix), row(g_mlp), row(b_gate)
    pscale_r, blc_r, ba_r, bx_r, lam_r = row(pool_scale), row(b_lru_conv), row(b_lru_a), row(b_lru_x), row(lru_lambda)
    bcc_r, gcf_r, bcf_r = row(b_conf_conv), row(g_conf), row(b_conf)

    xp, xs = x_prompt.reshape(n_prompt, d), x_sample.reshape(nb, d)
    tm = rows_p + nb // NT
    x = None
    outs = [[] for _ in range(4)]
    new_states = ()
    mix_w = (w_pool_grp, pscale_r, w_lru_conv, blc_r, w_lru_a, w_lru_x,
             ba_r, bx_r, lam_r, w_conf_conv, bcc_r, gcf_r, bcf_r)
    w_in_bf = _cast_bf16(w_in, rows=256)
    for l in range(depth):
        xn, f, p_last, l_last, h_last, v_last = _inmix(
            xp if l == 0 else x, g_mix_r, w_in_bf, *mix_w, l,
            batch=batch, seq=seq, rows_p=rows_p, tm=tm, tt=256, **widths)
        xn, zs = _inproj_sample(xn, xs if l == 0 else x, g_mix_r, w_in_bf, l, rows_p=rows_p)
        f, *new_states = _mixer_sample(
            f, zs, state_pool, state_lru_conv, state_lru_h, state_conf_conv, *mix_w, l,
            tuple(new_states),
            rows_p=rows_p, start_pos=PAST_LEN, **widths)
        merged = _gate_merge(xn, f, w_gate, b_gate_r, w_pool_br, w_lru_br, w_conf_br, l, tn=256)
        x = _outproj(merged, w_out, (xp, xs) if l == 0 else x, l, tn=1024)
        if l + 1 < depth:
            x = _mlp(x, g_mlp_r, w_up, w_down, l, tf=512)
        else:
            y_p, y_s = _mlp_final(x, g_mlp_r, w_up, w_down, g_final[None, :], l, tf=512,
                                  rows_p=rows_p)

        outs[0].append(p_last[:, POOL_HALO - pool_buf:])
        outs[1].append(l_last[:, LRU_HALO - lru_buf:])
        outs[2].append(h_last[:, SUBLANE - 1])
        outs[3].append(v_last[:, CONF_HALO - conf_buf:])

    return ((y_p.reshape(batch, seq, d), y_s.reshape(nb, 1, d))
            + tuple(jnp.stack(o) for o in outs) + tuple(new_states))
```

```python
import functools
import math

import jax
import jax.numpy as jnp
from jax import lax
from jax.experimental import pallas as pl
from jax.experimental.pallas import tpu as pltpu

F32 = jnp.float32
BF16 = jnp.bfloat16

POOL_WINDOWS = (2, 4, 8, 16)
LRU_C = 8.0
EPS = 1e-6
PAST_LEN = 16384
LANE = 128
SUBLANE = 8
NT = 8
VMEM_SLACK = 8 << 20
VMEM_CAP = 60 << 20


def _sigmoid(x):
    return 0.5 * jnp.tanh(0.5 * x) + 0.5


def _gelu_tanh(x):
    c = math.sqrt(2.0 / math.pi)
    return x * (0.5 * (1.0 + jnp.tanh(c * (x + 0.044715 * (x * x * x)))))


def _softplus(x):
    return jnp.maximum(x, 0.0) + jnp.log1p(jnp.exp(-jnp.abs(x)))


def _rms_norm(x, g):
    ms = jnp.mean(x * x, axis=-1, keepdims=True)
    return (x * lax.rsqrt(ms + EPS)) * g


def _layer_spec(a, l):
    zeros = (0,) * (a.ndim - 1)
    return pl.BlockSpec((None,) + a.shape[1:], lambda *_: (l,) + zeros)


def _params(vmem_bytes, n_axes):
    return pltpu.CompilerParams(
        dimension_semantics=("arbitrary",) * n_axes,
        vmem_limit_bytes=int(min(vmem_bytes + VMEM_SLACK, VMEM_CAP)))


def _cast_kernel(w_ref, o_ref):
    o_ref[...] = w_ref[...].astype(BF16)


def _cast_bf16(w, *, rows):
    depth, k, n = w.shape
    return pl.pallas_call(
        _cast_kernel,
        grid=(depth, k // rows),
        in_specs=[pl.BlockSpec((None, rows, n), lambda l, i: (l, i, 0))],
        out_specs=pl.BlockSpec((None, rows, n), lambda l, i: (l, i, 0)),
        out_shape=jax.ShapeDtypeStruct(w.shape, BF16),
        compiler_params=_params(2 * rows * n * 6, 2),
        name="cast_w_in",
    )(w)


def _lru_coeffs(ri, ba, bx, sp):
    gate_r = _sigmoid(ri[:, :LANE] + ba)
    gate_i = _sigmoid(ri[:, LANE:] + bx)
    log_a = (-LRU_C) * gate_r * sp
    a = jnp.exp(log_a)
    th = jnp.tanh(log_a)
    mult = jnp.sqrt((-2.0 * th) / (1.0 - th))
    return a, gate_i, mult


def _layer_norm_silu(vc, g, b):
    mu = jnp.mean(vc, axis=-1, keepdims=True)
    cen = vc - mu
    var = jnp.mean(cen * cen, axis=-1, keepdims=True)
    y = (cen * lax.rsqrt(var + EPS)) * g + b
    return y * _sigmoid(y)


def _scan_tile(a, b, carry, tt):
    groups = tt // SUBLANE
    a3 = a.reshape(groups, SUBLANE, LANE)
    b3 = b.reshape(groups, SUBLANE, LANE)
    sub = lax.broadcasted_iota(jnp.int32, (groups, SUBLANE, LANE), 1)
    for k in (1, 2, 4):
        keep = sub >= k
        b_sh = jnp.where(keep, pltpu.roll(b3, k, 1), 0.0)
        a_sh = jnp.where(keep, pltpu.roll(a3, k, 1), 1.0)
        b3 = b3 + a3 * b_sh
        a3 = a3 * a_sh
    hs = []
    for j in range(groups):
        hj = b3[j] + a3[j] * carry
        hs.append(hj)
        carry = hj[SUBLANE - 1:SUBLANE, :]
    return jnp.concatenate(hs, axis=0)


POOL_HALO = 16
LRU_HALO = 8
CONF_HALO = 32
LN_ROWS = 64


def _mix_reset(pool_ext, lru_ext, conf_ext, h_carry):
    pool_ext[0:POOL_HALO, :] = jnp.zeros((POOL_HALO, pool_ext.shape[1]), F32)
    lru_ext[0:LRU_HALO, :] = jnp.zeros((LRU_HALO, lru_ext.shape[1]), F32)
    conf_ext[0:CONF_HALO, :] = jnp.zeros((CONF_HALO, conf_ext.shape[1]), F32)
    h_carry[...] = jnp.zeros_like(h_carry)


def _mix_tile(t, z_ref, wgrp_ref, pscale_ref, wlc_ref, blc_ref, wa_ref, wx_ref, ba_ref,
              bx_ref, lam_ref, wcc_ref, bcc_ref, gcf_ref, bcf_ref,
              f_ref, plast_ref, llast_ref, hlast_ref, vlast_ref,
              pool_ext, lru_ext, conf_ext, vc_buf, h_carry,
              *, tt, pool_w, lru_w, conf_w, lru_taps, conf_taps, tick):
    c_lru = pool_w
    c_gel = c_lru + lru_w
    c_ca = c_gel + lru_w
    c_cb = c_ca + conf_w

    row = lax.broadcasted_iota(jnp.int32, (tt, LANE), 0)
    pos = row + t * tt
    posf = pos.astype(F32)
    is_first = pos == 0
    never = posf[0:SUBLANE, :] < 0.0

    tick()
    pool_ext[POOL_HALO:POOL_HALO + tt, :] = z_ref[:, 0:pool_w]
    for g, w in enumerate(POOL_WINDOWS):
        lanes = slice(g * LANE, (g + 1) * LANE)
        u = pool_ext[POOL_HALO:POOL_HALO + tt, lanes]
        s = u
        for j in range(1, w):
            s = s + pool_ext[POOL_HALO - j:POOL_HALO - j + tt, lanes]
        cnt = jnp.minimum(jnp.float32(w), posf + 1.0)
        d = s / cnt - u
        y = jnp.dot(d.astype(BF16), wgrp_ref[g].astype(BF16), preferred_element_type=F32)
        f_ref[:, lanes] = (y * pscale_ref[:, lanes]).astype(BF16)
    plast_ref[...] = pool_ext[tt:tt + POOL_HALO, :]
    pool_ext[0:POOL_HALO, :] = pool_ext[tt:tt + POOL_HALO, :]

    lru_ext[LRU_HALO:LRU_HALO + tt, :] = z_ref[:, c_lru:c_lru + lru_w]
    sp_all = _softplus(-lam_ref[...])
    for n in range(lru_w // LANE):
        lanes = slice(n * LANE, (n + 1) * LANE)
        xc = blc_ref[:, lanes] + wlc_ref[lru_taps - 1:lru_taps, lanes] * lru_ext[LRU_HALO:LRU_HALO + tt, lanes]
        for k in range(lru_taps - 1):
            off = LRU_HALO - (lru_taps - 1) + k
            xc = xc + wlc_ref[k:k + 1, lanes] * lru_ext[off:off + tt, lanes]
        xcb = xc.astype(BF16)
        ri = jnp.concatenate(
            [jnp.dot(xcb, wa_ref[n].astype(BF16), preferred_element_type=F32),
             jnp.dot(xcb, wx_ref[n].astype(BF16), preferred_element_type=F32)], axis=1)
        tick()
        a, gate_i, mult = _lru_coeffs(ri, ba_ref[:, lanes], bx_ref[:, lanes], sp_all[:, lanes])
        a = jnp.where(is_first, 0.0, a)
        mult = jnp.where(is_first, 1.0, mult)
        b = xc * gate_i * mult
        h = _scan_tile(a, b, h_carry[SUBLANE - 1:SUBLANE, lanes], tt)
        h_carry[:, lanes] = h[tt - SUBLANE:tt, :]
        hlast_ref[:, lanes] = h[tt - SUBLANE:tt, :]
        gel = _gelu_tanh(z_ref[:, c_gel + n * LANE:c_gel + (n + 1) * LANE])
        f_ref[:, pool_w + n * LANE:pool_w + (n + 1) * LANE] = (h * gel).astype(BF16)
    llast_ref[...] = lru_ext[tt:tt + LRU_HALO, :]
    lru_ext[0:LRU_HALO, :] = lru_ext[tt:tt + LRU_HALO, :]

    conf_ext[CONF_HALO:CONF_HALO + tt, :] = (
        z_ref[:, c_ca:c_ca + conf_w] * _sigmoid(z_ref[:, c_cb:c_cb + conf_w]))
    base = CONF_HALO - (conf_taps - 1)
    for c in range(conf_w // LANE):
        lanes = slice(c * LANE, (c + 1) * LANE)
        tick()
        acc = None
        for r in range(SUBLANE):
            offs = [o for o in range(base, CONF_HALO + 1) if o % SUBLANE == r]
            if not offs:
                continue
            part = None
            for o in offs:
                k = o - base
                term = wcc_ref[k:k + 1, lanes] * conf_ext[o:o + tt, lanes]
                part = term if part is None else part + term
            acc = part if acc is None else acc + part
        vc = acc + bcc_ref[:, lanes]
        pace = jnp.dot(vc[0:SUBLANE, :].astype(BF16), wgrp_ref[0].astype(BF16),
                       preferred_element_type=F32)
        vc_buf[0:SUBLANE, lanes] = jnp.where(never, pace, vc[0:SUBLANE, :])
        vc_buf[SUBLANE:, lanes] = vc[SUBLANE:, :]
    tick()
    for r in range(tt // LN_ROWS):
        rows = slice(r * LN_ROWS, (r + 1) * LN_ROWS)
        y = _layer_norm_silu(vc_buf[rows, :], gcf_ref[...], bcf_ref[...])
        f_ref[rows, pool_w + lru_w:pool_w + lru_w + conf_w] = y.astype(BF16)
    vlast_ref[...] = conf_ext[tt:tt + CONF_HALO, :]
    conf_ext[0:CONF_HALO, :] = conf_ext[tt:tt + CONF_HALO, :]


def _inmix_kernel(x_ref, g_ref, w_ref, *refs, n_tiles, steps_per_seq, mix):
    n_w = 13
    weights = refs[:n_w]
    xn_ref, f_ref, plast_ref, llast_ref, hlast_ref, vlast_ref = refs[n_w:n_w + 6]
    z_a, z_b, pool_ext, lru_ext, conf_ext, vc_buf, h_carry = refs[n_w + 6:]
    s = pl.program_id(0)
    t_prev = lax.rem(s + (steps_per_seq - 1), steps_per_seq)

    @pl.when(s == 0)
    def _():
        z_b[...] = jnp.zeros_like(z_b)
        _mix_reset(pool_ext, lru_ext, conf_ext, h_carry)

    @pl.when(jnp.logical_and(s > 0, t_prev == 0))
    def _():
        _mix_reset(pool_ext, lru_ext, conf_ext, h_carry)

    zc = w_ref.shape[1]
    sections = 1 + mix["lru_w"] // LANE + mix["conf_w"] // LANE + 1
    cw = zc // sections

    def step(z_w, z_r):
        xn_ref[...] = _rms_norm(x_ref[...], g_ref[...]).astype(BF16)
        done = [0]

        def tick():
            c0 = done[0] * cw
            z_w[:, c0:c0 + cw] = jnp.dot(xn_ref[...], w_ref[:, c0:c0 + cw],
                                         preferred_element_type=F32)
            done[0] += 1

        _mix_tile(t_prev, z_r, *weights, f_ref, plast_ref, llast_ref, hlast_ref, vlast_ref,
                  pool_ext, lru_ext, conf_ext, vc_buf, h_carry, tick=tick, **mix)
        assert done[0] * cw == zc

    @pl.when(lax.rem(s, 2) == 0)
    def _():
        step(z_a, z_b)

    @pl.when(lax.rem(s, 2) == 1)
    def _():
        step(z_b, z_a)


def _inmix(x, g, w_bf, wgrp, pscale, wlc, blc, wa, wx, ba, bx, lam, wcc, bcc, gcf, bcf, l,
           *, batch, seq, rows_p, tm, tt, pool_w, lru_w, conf_w):
    d = x.shape[-1]
    nt = batch * seq // rows_p
    zc = w_bf.shape[2]
    steps_per_tile = rows_p // tt
    steps_per_seq = seq // tt
    n_tiles = batch * steps_per_seq
    fw = pool_w + lru_w + conf_w
    mix = dict(tt=tt, pool_w=pool_w, lru_w=lru_w, conf_w=conf_w,
               lru_taps=wlc.shape[1], conf_taps=wcc.shape[1])

    def cur(s):
        c = jnp.minimum(s, n_tiles - 1)
        return (c // steps_per_tile, c % steps_per_tile, 0)

    def prev(s):
        p = jnp.maximum(s - 1, 0)
        return (p // steps_per_tile, p % steps_per_tile, 0)

    smap = lambda s: (jnp.maximum(s - 1, 0) // steps_per_seq, 0, 0)
    vmem = (d * zc * 2 + 2 * tt * zc * 4 + 2 * tt * d * 4 + 2 * tt * d * 2 + 2 * tt * fw * 2
            + (POOL_HALO + tt) * pool_w * 4 + (LRU_HALO + tt) * lru_w * 4
            + (CONF_HALO + tt) * conf_w * 4 + tt * conf_w * 4 + (10 << 20))
    weights = (wgrp, pscale, wlc, blc, wa, wx, ba, bx, lam, wcc, bcc, gcf, bcf)
    x_spec = (pl.BlockSpec((tt, d), lambda s: (jnp.minimum(s, n_tiles - 1), 0)) if x.ndim == 2
              else pl.BlockSpec((None, tt, d), cur))
    return pl.pallas_call(
        functools.partial(_inmix_kernel, n_tiles=n_tiles, steps_per_seq=steps_per_seq, mix=mix),
        grid=(n_tiles + 1,),
        in_specs=[x_spec,
                  _layer_spec(g, l),
                  pl.BlockSpec((None, d, zc), lambda s: (l, 0, 0), pipeline_mode=pl.Buffered(1))]
                 + [_layer_spec(a, l) for a in weights],
        out_specs=[
            pl.BlockSpec((None, tt, d), cur),
            pl.BlockSpec((None, tt, fw), prev),
            pl.BlockSpec((None, POOL_HALO, pool_w), smap),
            pl.BlockSpec((None, LRU_HALO, lru_w), smap),
            pl.BlockSpec((None, SUBLANE, lru_w), smap),
            pl.BlockSpec((None, CONF_HALO, conf_w), smap),
        ],
        out_shape=[
            jax.ShapeDtypeStruct((nt, tm, d), BF16),
            jax.ShapeDtypeStruct((nt, tm, fw), BF16),
            jax.ShapeDtypeStruct((batch, POOL_HALO, pool_w), F32),
            jax.ShapeDtypeStruct((batch, LRU_HALO, lru_w), F32),
            jax.ShapeDtypeStruct((batch, SUBLANE, lru_w), F32),
            jax.ShapeDtypeStruct((batch, CONF_HALO, conf_w), F32),
        ],
        scratch_shapes=[
            pltpu.VMEM((tt, zc), F32),
            pltpu.VMEM((tt, zc), F32),
            pltpu.VMEM((POOL_HALO + tt, pool_w), F32),
            pltpu.VMEM((LRU_HALO + tt, lru_w), F32),
            pltpu.VMEM((CONF_HALO + tt, conf_w), F32),
            pltpu.VMEM((tt, conf_w), F32),
            pltpu.VMEM((SUBLANE, lru_w), F32),
        ],
        compiler_params=_params(vmem, 1),
        name="inmix",
    )(x, g, w_bf, *weights)


def _inproj_sample_kernel(xn_any, x_ref, g_ref, w_ref, xn_ref, zs_ref, xs_buf):
    del xn_any
    i = pl.program_id(0)
    rs = x_ref.shape[0]
    xn = _rms_norm(x_ref[...], g_ref[...]).astype(BF16)
    xn_ref[...] = xn
    xs_buf[pl.ds(pl.multiple_of(i * rs, rs), rs), :] = xn

    @pl.when(i == pl.num_programs(0) - 1)
    def _():
        zs_ref[...] = jnp.dot(xs_buf[...], w_ref[...], preferred_element_type=F32)


def _inproj_sample(xn, x, g, w_bf, l, *, rows_p):
    nt, tm, d = xn.shape
    rs = tm - rows_p
    zc = w_bf.shape[2]
    rowmap = lambda i: (i, rows_p // rs, 0)
    x_spec = (pl.BlockSpec((rs, d), lambda i: (i, 0)) if x.ndim == 2
              else pl.BlockSpec((None, rs, d), rowmap))
    vmem = d * zc * 2 + 2 * nt * rs * zc * 4 + nt * rs * d * 2 + (4 << 20)
    return pl.pallas_call(
        _inproj_sample_kernel,
        grid=(nt,),
        in_specs=[pl.BlockSpec(memory_space=pl.ANY),
                  x_spec,
                  _layer_spec(g, l),
                  pl.BlockSpec((None, d, zc), lambda i: (l, 0, 0), pipeline_mode=pl.Buffered(1))],
        out_specs=[pl.BlockSpec((None, rs, d), rowmap),
                   pl.BlockSpec((nt * rs, zc), lambda i: (0, 0))],
        out_shape=[jax.ShapeDtypeStruct(xn.shape, xn.dtype),
                   jax.ShapeDtypeStruct((nt * rs, zc), F32)],
        input_output_aliases={0: 0},
        scratch_shapes=[pltpu.VMEM((nt * rs, d), BF16)],
        compiler_params=_params(vmem, 1),
        name="inproj_sample",
    )(xn, x, g, w_bf)


def _mixer_sample_kernel(f_any, z_ref, spool_ref, slru_ref, sh_ref, sconf_ref,
                         wgrp_ref, pscale_ref, wlc_ref, blc_ref, wa_ref, wx_ref, ba_ref,
                         bx_ref, lam_ref, wcc_ref, bcc_ref, gcf_ref, bcf_ref, *rest,
                         pool_w, lru_w, conf_w, lru_taps, conf_taps, pool_buf, start_pos):
    f_ref, npool_ref, nlru_ref, hnew_ref, nconf_ref, vc_buf = rest[-6:]
    del f_any
    c_lru = pool_w
    c_gel = c_lru + lru_w
    c_ca = c_gel + lru_w
    c_cb = c_ca + conf_w

    for g, w in enumerate(POOL_WINDOWS):
        lanes = slice(g * LANE, (g + 1) * LANE)
        u = z_ref[:, lanes]
        s = u
        for j in range(1, w):
            trow = pool_buf - j
            s = s + spool_ref[:, trow, lanes]
        cnt = float(min(w, start_pos + 1))
        d = s / cnt - u
        y = jnp.dot(d.astype(BF16), wgrp_ref[g].astype(BF16), preferred_element_type=F32)
        f_ref[:, lanes] = (y * pscale_ref[:, lanes]).astype(BF16)
    npool_ref[:, 0:pool_buf - 1, :] = spool_ref[:, 1:pool_buf, :]
    npool_ref[:, pool_buf - 1, :] = z_ref[:, 0:pool_w]

    sp_all = _softplus(-lam_ref[...])
    for n in range(lru_w // LANE):
        lanes = slice(n * LANE, (n + 1) * LANE)
        xc = blc_ref[:, lanes] + wlc_ref[lru_taps - 1:lru_taps, lanes] * z_ref[:, c_lru + n * LANE:c_lru + (n + 1) * LANE]
        for k in range(lru_taps - 1):
            xc = xc + wlc_ref[k:k + 1, lanes] * slru_ref[:, k, lanes]
        xcb = xc.astype(BF16)
        ri = jnp.concatenate(
            [jnp.dot(xcb, wa_ref[n].astype(BF16), preferred_element_type=F32),
             jnp.dot(xcb, wx_ref[n].astype(BF16), preferred_element_type=F32)], axis=1)
        a, gate_i, mult = _lru_coeffs(ri, ba_ref[:, lanes], bx_ref[:, lanes], sp_all[:, lanes])
        if start_pos == 0:
            a = jnp.zeros_like(a)
            mult = jnp.ones_like(mult)
        h = a * sh_ref[:, lanes] + xc * gate_i * mult
        hnew_ref[:, lanes] = h
        gel = _gelu_tanh(z_ref[:, c_gel + n * LANE:c_gel + (n + 1) * LANE])
        f_ref[:, pool_w + n * LANE:pool_w + (n + 1) * LANE] = (h * gel).astype(BF16)
    nlru_ref[:, 0:lru_taps - 2, :] = slru_ref[:, 1:lru_taps - 1, :]
    nlru_ref[:, lru_taps - 2, :] = z_ref[:, c_lru:c_lru + lru_w]

    for c in range(conf_w // LANE):
        lanes = slice(c * LANE, (c + 1) * LANE)
        v = z_ref[:, c_ca + c * LANE:c_ca + (c + 1) * LANE] * _sigmoid(
            z_ref[:, c_cb + c * LANE:c_cb + (c + 1) * LANE])
        nconf_ref[:, conf_taps - 2, lanes] = v
        acc = bcc_ref[:, lanes] + wcc_ref[conf_taps - 1:conf_taps, lanes] * v
        for k in range(conf_taps - 1):
            acc = acc + wcc_ref[k:k + 1, lanes] * sconf_ref[:, k, lanes]
        vc_buf[:, lanes] = acc
    nconf_ref[:, 0:conf_taps - 2, :] = sconf_ref[:, 1:conf_taps - 1, :]
    y = _layer_norm_silu(vc_buf[...], gcf_ref[...], bcf_ref[...])
    f_ref[:, pool_w + lru_w:pool_w + lru_w + conf_w] = y.astype(BF16)


def _mixer_sample(f, z, spool, slru, sh, sconf, wgrp, pscale, wlc, blc, wa, wx, ba, bx, lam,
                  wcc, bcc, gcf, bcf, l, new_states, *, rows_p, pool_w, lru_w, conf_w, start_pos):
    nt, tm, _ = f.shape
    zc = z.shape[1]
    rs = tm - rows_p
    fw = pool_w + lru_w + conf_w
    lru_taps = wlc.shape[1]
    conf_taps = wcc.shape[1]
    pool_buf = spool.shape[2]
    rowmap = lambda i: (i, rows_p // rs, 0)
    bmap = lambda i: (i, 0)
    states = (spool, slru, sh, sconf)
    weights = (wgrp, pscale, wlc, blc, wa, wx, ba, bx, lam, wcc, bcc, gcf, bcf)
    n_in = 2 + len(states) + len(weights)

    def state_spec(s):
        zeros = (0,) * (s.ndim - 2)
        return pl.BlockSpec((None, rs) + s.shape[2:], lambda i: (l, i) + zeros)

    vmem = (4 * rs * sum(math.prod(s.shape[2:]) for s in states) * 4 * 2
            + 2 * sum(a[0].size * a.dtype.itemsize for a in weights) + (8 << 20))
    kern = functools.partial(_mixer_sample_kernel, pool_w=pool_w, lru_w=lru_w, conf_w=conf_w,
                             lru_taps=lru_taps, conf_taps=conf_taps, pool_buf=pool_buf,
                             start_pos=start_pos)
    return pl.pallas_call(
        kern,
        grid=(nt,),
        in_specs=[pl.BlockSpec(memory_space=pl.ANY),
                  pl.BlockSpec((rs, zc), bmap)]
                 + [state_spec(s) for s in states]
                 + [_layer_spec(a, l) for a in weights]
                 + [pl.BlockSpec(memory_space=pl.ANY) for _ in new_states],
        out_specs=[pl.BlockSpec((None, rs, fw), rowmap)] + [state_spec(s) for s in states],
        out_shape=[jax.ShapeDtypeStruct(f.shape, f.dtype)]
                  + [jax.ShapeDtypeStruct(s.shape, F32) for s in states],
        input_output_aliases={0: 0, **{n_in + k: 1 + k for k in range(len(new_states))}},
        scratch_shapes=[pltpu.VMEM((rs, conf_w), F32)],
        compiler_params=_params(vmem, 1),
        name="mixer_sample",
    )(f, z, *states, *weights, *new_states)


def _gate_merge_kernel(xn_ref, f_ref, wga_ref, wgb_ref, wgc_ref, bga_ref, bgb_ref, bgc_ref,
                       wp_ref, wl_ref, wc_ref, o_ref, wg_bf, wb_bf, *, pool_w, lru_w):
    c1 = pool_w
    c2 = pool_w + lru_w

    @pl.when(pl.program_id(1) == 0)
    def _():
        wg_bf[0] = wga_ref[...].astype(BF16)
        wg_bf[1] = wgb_ref[...].astype(BF16)
        wg_bf[2] = wgc_ref[...].astype(BF16)
        wb_bf[0:c1, :] = wp_ref[...].astype(BF16)
        wb_bf[c1:c2, :] = wl_ref[...].astype(BF16)
        wb_bf[c2:, :] = wc_ref[...].astype(BF16)

    xn = xn_ref[...]
    ga = _sigmoid(jnp.dot(xn, wg_bf[0], preferred_element_type=F32) + bga_ref[...])
    m = ga * jnp.dot(f_ref[:, 0:c1], wb_bf[0:c1, :], preferred_element_type=F32)
    gb = _sigmoid(jnp.dot(xn, wg_bf[1], preferred_element_type=F32) + bgb_ref[...])
    m = m + gb * jnp.dot(f_ref[:, c1:c2], wb_bf[c1:c2, :], preferred_element_type=F32)
    gc = _sigmoid(jnp.dot(xn, wg_bf[2], preferred_element_type=F32) + bgc_ref[...])
    m = m + gc * jnp.dot(f_ref[:, c2:], wb_bf[c2:, :], preferred_element_type=F32)
    o_ref[...] = m.astype(BF16)


def _gate_merge(xn, f, wgate, bgate, wp, wl, wc, l, *, tn):
    nt, tm, d = xn.shape
    fw = f.shape[2]
    pool_w, lru_w = wp.shape[1], wl.shape[1]
    nd = d // tn
    vmem = (2 * tm * (d + fw) * 2 + 2 * (3 * d + fw) * tn * 4 + (3 * d + fw) * tn * 2
            + 2 * tm * tn * 2 + 6 * tm * tn * 4)
    return pl.pallas_call(
        functools.partial(_gate_merge_kernel, pool_w=pool_w, lru_w=lru_w),
        grid=(nd, nt),
        in_specs=[
            pl.BlockSpec((None, tm, d), lambda j, i: (i, 0, 0)),
            pl.BlockSpec((None, tm, fw), lambda j, i: (i, 0, 0)),
            pl.BlockSpec((None, d, tn), lambda j, i: (l, 0, j)),
            pl.BlockSpec((None, d, tn), lambda j, i: (l, 0, nd + j)),
            pl.BlockSpec((None, d, tn), lambda j, i: (l, 0, 2 * nd + j)),
            pl.BlockSpec((None, 1, tn), lambda j, i: (l, 0, j)),
            pl.BlockSpec((None, 1, tn), lambda j, i: (l, 0, nd + j)),
            pl.BlockSpec((None, 1, tn), lambda j, i: (l, 0, 2 * nd + j)),
            pl.BlockSpec((None, pool_w, tn), lambda j, i: (l, 0, j)),
            pl.BlockSpec((None, lru_w, tn), lambda j, i: (l, 0, j)),
            pl.BlockSpec((None, fw - pool_w - lru_w, tn), lambda j, i: (l, 0, j)),
        ],
        out_specs=pl.BlockSpec((None, tm, tn), lambda j, i: (i, 0, j)),
        out_shape=jax.ShapeDtypeStruct((nt, tm, d), BF16),
        scratch_shapes=[pltpu.VMEM((3, d, tn), BF16), pltpu.VMEM((fw, tn), BF16)],
        compiler_params=_params(vmem, 2),
        name="gate_merge",
    )(xn, f, wgate, wgate, wgate, bgate, bgate, bgate, wp, wl, wc)


def _outproj_kernel(m_ref, w_ref, *refs):
    *x_refs, o_ref, w_bf = refs

    @pl.when(pl.program_id(1) == 0)
    def _():
        w_bf[...] = w_ref[...].astype(BF16)

    acc = jnp.dot(m_ref[...], w_bf[...], preferred_element_type=F32)
    row = 0
    for x_ref in x_refs:
        rows = x_ref.shape[0]
        o_ref[row:row + rows, :] = x_ref[...] + acc[row:row + rows, :]
        row += rows


def _outproj(m, w, x, l, *, tn):
    nt, tm, k = m.shape
    d = w.shape[2]
    if isinstance(x, tuple):
        xs_in = x
        x_specs = [pl.BlockSpec((a.shape[0] // nt, tn), lambda j, i: (i, j)) for a in x]
    else:
        xs_in = (x,)
        x_specs = [pl.BlockSpec((None, tm, tn), lambda j, i: (i, 0, j))]
    vmem = 2 * tm * k * 2 + 2 * k * tn * 4 + k * tn * 2 + 4 * tm * tn * 4 + tm * tn * 4
    return pl.pallas_call(
        _outproj_kernel,
        grid=(d // tn, nt),
        in_specs=[
            pl.BlockSpec((None, tm, k), lambda j, i: (i, 0, 0)),
            pl.BlockSpec((None, k, tn), lambda j, i: (l, 0, j)),
        ] + x_specs,
        out_specs=pl.BlockSpec((None, tm, tn), lambda j, i: (i, 0, j)),
        out_shape=jax.ShapeDtypeStruct((nt, tm, d), F32),
        scratch_shapes=[pltpu.VMEM((k, tn), BF16)],
        compiler_params=_params(vmem, 2),
        name="outproj",
    )(m, w, *xs_in)


def _mlp_kernel(x_ref, g_ref, wup_ref, wdn_ref, o_ref, xn_ref):
    @pl.when(pl.program_id(1) == 0)
    def _():
        x = x_ref[...]
        xn_ref[...] = _rms_norm(x, g_ref[...]).astype(BF16)
        o_ref[...] = x

    hid = jnp.dot(xn_ref[...], wup_ref[...].astype(BF16), preferred_element_type=F32)
    hid = jnp.maximum(hid, 0.0)
    hid = (hid * hid).astype(BF16)
    o_ref[...] += jnp.dot(hid, wdn_ref[...].astype(BF16), preferred_element_type=F32)


def _mlp(x, g, wup, wdn, l, *, tf):
    nt, tm, d = x.shape
    dff = wup.shape[2]
    vmem = (4 * tm * d * 4 + tm * d * 2 + 4 * d * tf * 4 + 2 * d * tf * 2 + 2 * tm * tf * 4)
    return pl.pallas_call(
        _mlp_kernel,
        grid=(nt, dff // tf),
        in_specs=[
            pl.BlockSpec((None, tm, d), lambda i, j: (i, 0, 0)),
            _layer_spec(g, l),
            pl.BlockSpec((None, d, tf), lambda i, j: (l, 0, j)),
            pl.BlockSpec((None, tf, d), lambda i, j: (l, j, 0)),
        ],
        out_specs=pl.BlockSpec((None, tm, d), lambda i, j: (i, 0, 0)),
        out_shape=jax.ShapeDtypeStruct((nt, tm, d), F32),
        scratch_shapes=[pltpu.VMEM((tm, d), BF16)],
        compiler_params=_params(vmem, 2),
        name="mlp",
    )(x, g, wup, wdn)


def _mlp_final_kernel(x_ref, g_ref, wup_ref, wdn_ref, gf_ref, yp_ref, ys_ref, xn_ref):
    rp = yp_ref.shape[0]

    @pl.when(pl.program_id(1) == 0)
    def _():
        x = x_ref[...]
        xn_ref[...] = _rms_norm(x, g_ref[...]).astype(BF16)
        yp_ref[...] = x[0:rp, :]
        ys_ref[...] = x[rp:, :]

    hid = jnp.dot(xn_ref[...], wup_ref[...].astype(BF16), preferred_element_type=F32)
    hid = jnp.maximum(hid, 0.0)
    hid = (hid * hid).astype(BF16)
    contrib = jnp.dot(hid, wdn_ref[...].astype(BF16), preferred_element_type=F32)
    yp_ref[...] += contrib[0:rp, :]
    ys_ref[...] += contrib[rp:, :]

    @pl.when(pl.program_id(1) == pl.num_programs(1) - 1)
    def _():
        yp_ref[...] = _rms_norm(yp_ref[...], gf_ref[...])
        ys_ref[...] = _rms_norm(ys_ref[...], gf_ref[...])


def _mlp_final(x, g, wup, wdn, g_final, l, *, tf, rows_p):
    nt, tm, d = x.shape
    rs = tm - rows_p
    dff = wup.shape[2]
    vmem = (4 * tm * d * 4 + tm * d * 2 + 4 * d * tf * 4 + 2 * d * tf * 2 + 2 * tm * tf * 4)
    return pl.pallas_call(
        _mlp_final_kernel,
        grid=(nt, dff // tf),
        in_specs=[
            pl.BlockSpec((None, tm, d), lambda i, j: (i, 0, 0)),
            _layer_spec(g, l),
            pl.BlockSpec((None, d, tf), lambda i, j: (l, 0, j)),
            pl.BlockSpec((None, tf, d), lambda i, j: (l, j, 0)),
            pl.BlockSpec((1, d), lambda i, j: (0, 0)),
        ],
        out_specs=[pl.BlockSpec((rows_p, d), lambda i, j: (i, 0)),
                   pl.BlockSpec((rs, d), lambda i, j: (i, 0))],
        out_shape=[jax.ShapeDtypeStruct((nt * rows_p, d), F32),
                   jax.ShapeDtypeStruct((nt * rs, d), F32)],
        scratch_shapes=[pltpu.VMEM((tm, d), BF16)],
        compiler_params=_params(vmem, 2),
        name="mlp_final",
    )(x, g, wup, wdn, g_final)


def kernel(x_prompt, x_sample, state_pool, state_lru_conv, state_lru_h, state_conf_conv, g_mix, w_in, w_pool_grp, pool_scale, w_pool_br, w_lru_conv, b_lru_conv, w_lru_a, b_lru_a, w_lru_x, b_lru_x, lru_lambda, w_lru_br, w_conf_conv, b_conf_conv, g_conf, b_conf, w_conf_br, w_gate, b_gate, w_out, g_mlp, w_up, w_down, g_final):
    batch, seq, d = x_prompt.shape
    nb, dec_seq, _ = x_sample.shape
    assert dec_seq == 1
    depth = w_in.shape[0]
    pool_w = w_pool_br.shape[1]
    lru_w = w_lru_br.shape[1]
    conf_w = w_conf_br.shape[1]
    pool_buf = state_pool.shape[2]
    lru_buf = state_lru_conv.shape[2]
    conf_buf = state_conf_conv.shape[2]
    n_prompt = batch * seq
    rows_p = n_prompt // NT
    widths = dict(pool_w=pool_w, lru_w=lru_w, conf_w=conf_w)

    row = lambda a: a[:, None, :]
    g_mix_r, g_mlp_r, b_gate_r = row(g_mix), row(g_mlp), row(b_gate)
    pscale_r, blc_r, ba_r, bx_r, lam_r = row(pool_scale), row(b_lru_conv), row(b_lru_a), row(b_lru_x), row(lru_lambda)
    bcc_r, gcf_r, bcf_r = row(b_conf_conv), row(g_conf), row(b_conf)

    xp, xs = x_prompt.reshape(n_prompt, d), x_sample.reshape(nb, d)
    tm = rows_p + nb // NT
    x = None
    outs = [[] for _ in range(4)]
    new_states = ()
    mix_w = (w_pool_grp, pscale_r, w_lru_conv, blc_r, w_lru_a, w_lru_x,
             ba_r, bx_r, lam_r, w_conf_conv, bcc_r, gcf_r, bcf_r)
    w_in_bf = _cast_bf16(w_in, rows=256)
    for l in range(depth):
        xn, f, p_last, l_last, h_last, v_last = _inmix(
            xp if l == 0 else x, g_mix_r, w_in_bf, *mix_w, l,
            batch=batch, seq=seq, rows_p=rows_p, tm=tm, tt=256, **widths)
        xn, zs = _inproj_sample(xn, xs if l == 0 else x, g_mix_r, w_in_bf, l, rows_p=rows_p)
        f, *new_states = _mixer_sample(
            f, zs, state_pool, state_lru_conv, state_lru_h, state_conf_conv, *mix_w, l,
            tuple(new_states),
            rows_p=rows_p, start_pos=PAST_LEN, **widths)
        merged = _gate_merge(xn, f, w_gate, b_gate_r, w_pool_br, w_lru_br, w_conf_br, l, tn=256)
        x = _outproj(merged, w_out, (xp, xs) if l == 0 else x, l, tn=1024)
        if l + 1 < depth:
            x = _mlp(x, g_mlp_r, w_up, w_down, l, tf=512)
        else:
            y_p, y_s = _mlp_final(x, g_mlp_r, w_up, w_down, g_final[None, :], l, tf=512,
                                  rows_p=rows_p)

        outs[0].append(p_last[:, POOL_HALO - pool_buf:])
        outs[1].append(l_last[:, LRU_HALO - lru_buf:])
        outs[2].append(h_last[:, SUBLANE - 1])
        outs[3].append(v_last[:, CONF_HALO - conf_buf:])

    return ((y_p.reshape(batch, seq, d), y_s.reshape(nb, 1, d))
            + tuple(jnp.stack(o) for o in outs) + tuple(new_states))
```
